```python
import math
import jax
import jax.numpy as jnp
from jax import lax
import numpy as np

D_MODEL = 1024
BATCH = 16
SEQ = 4096
DEPTH = 4
DEC_BATCH = 4
DEC_SEQ = 8192
PAST_LEN = 128

GRID_W = 64
Q_BLOCK = 128
N_MIXERS = 3
NORM_EPS = 1e-6

MLA_HEADS = 16
MLA_Q_LORA = 256
MLA_KV_LORA = 128
MLA_NOPE = 64
MLA_ROPE = 32
MLA_V = 64
MLA_WIDTH = MLA_HEADS * MLA_V
MLA_IN_WIDTH = MLA_Q_LORA + MLA_KV_LORA + MLA_ROPE + MLA_WIDTH
MLA_ROPE_THETA = 10000.0

GQA_HEADS = 16
GQA_KV_HEADS = 4
GQA_GROUP = GQA_HEADS // GQA_KV_HEADS
GQA_HD = 64
GQA_WIDTH = GQA_HEADS * GQA_HD
GQA_KV_WIDTH = GQA_KV_HEADS * GQA_HD
GQA_IN_WIDTH = GQA_WIDTH + 2 * GQA_KV_WIDTH + GQA_WIDTH
GQA_ROPE_THETA = 10000.0

DIFF_HEADS = 8
DIFF_HD = 64
DIFF_QK_WIDTH = DIFF_HEADS * 2 * DIFF_HD
DIFF_WIDTH = DIFF_HEADS * 2 * DIFF_HD
DIFF_IN_WIDTH = 2 * DIFF_QK_WIDTH + 2 * DIFF_WIDTH
DIFF_LAMBDA_STD = 0.1

kernel_name = "hybrid_mla_gqa_diff_encoder"


def rms_norm(x, g, eps=NORM_EPS):
    xf = x.astype(jnp.float32)
    y = xf * lax.rsqrt(jnp.mean(xf * xf, axis=-1, keepdims=True) + eps)
    return (y * g.astype(jnp.float32)).astype(x.dtype)


def rope_angles(pos, dim, theta):
    inv = 1.0 / (theta ** (jnp.arange(0, dim, 2, dtype=jnp.float32) / dim))
    ang = pos[:, None] * inv[None, :]
    return jnp.cos(ang), jnp.sin(ang)


def apply_rope(x, cos, sin):
    d2 = x.shape[-1] // 2
    x1, x2 = x[..., :d2], x[..., d2:]
    c = cos[:, None, :]
    s = sin[:, None, :]
    return jnp.concatenate([x1 * c - x2 * s, x1 * s + x2 * c], axis=-1).astype(x.dtype)


def alibi_slopes(n):
    return jnp.exp2(-8.0 * jnp.arange(1, n + 1, dtype=jnp.float32) / n)


def sweep_query_blocks(fn, *q_arrays):
    B, S = q_arrays[0].shape[:2]
    nb = S // Q_BLOCK
    blocked = [jnp.moveaxis(a.reshape((B, nb, Q_BLOCK) + a.shape[2:]), 1, 0) for a in q_arrays]
    out = lax.map(lambda args: fn(args[0], *args[1]), (jnp.arange(nb), blocked))
    return jnp.moveaxis(out, 0, 1).reshape((B, S) + out.shape[3:])


def mla_mixer(h, p):
    B, S, _ = h.shape
    f32 = jnp.float32
    proj = h @ p["w_in"]
    c_q, c_kv, k_rope, gate = jnp.split(
        proj, [MLA_Q_LORA, MLA_Q_LORA + MLA_KV_LORA, MLA_Q_LORA + MLA_KV_LORA + MLA_ROPE], axis=-1)
    q = (rms_norm(c_q, p["q_norm"]) @ p["w_uq"]).reshape(B, S, MLA_HEADS, MLA_NOPE + MLA_ROPE)
    kv = (rms_norm(c_kv, p["kv_norm"]) @ p["w_ukv"]).reshape(B, S, MLA_HEADS, MLA_NOPE + MLA_V)
    q_nope, q_rope = q[..., :MLA_NOPE], q[..., MLA_NOPE:]
    k_nope, v = kv[..., :MLA_NOPE], kv[..., MLA_NOPE:]
    cos, sin = rope_angles(jnp.arange(S, dtype=f32), MLA_ROPE, MLA_ROPE_THETA)
    q_rope = apply_rope(q_rope, cos, sin)
    k_rope = apply_rope(k_rope[:, :, None, :], cos, sin)[:, :, 0]
    scale = (MLA_NOPE + MLA_ROPE) ** -0.5

    def block(i, qn, qr):
        s = (jnp.einsum("bqhd,bkhd->bhqk", qn, k_nope)
             + jnp.einsum("bqhr,bkr->bhqk", qr, k_rope)).astype(f32) * scale
        a = jax.nn.softmax(s, axis=-1).astype(v.dtype)
        return jnp.einsum("bhqk,bkhd->bqhd", a, v).reshape(B, Q_BLOCK, MLA_WIDTH)

    o = sweep_query_blocks(block, q_nope, q_rope)
    return (o * jax.nn.silu(gate)) @ p["w_out"]


def gqa_mixer(h, p):
    B, S, _ = h.shape
    f32 = jnp.float32
    proj = h @ p["w_in"]
    q, k, v, gate = jnp.split(
        proj, [GQA_WIDTH, GQA_WIDTH + GQA_KV_WIDTH, GQA_WIDTH + 2 * GQA_KV_WIDTH], axis=-1)
    q = rms_norm(q.reshape(B, S, GQA_HEADS, GQA_HD), p["q_norm"])
    k = rms_norm(k.reshape(B, S, GQA_KV_HEADS, GQA_HD), p["k_norm"])
    v = v.reshape(B, S, GQA_KV_HEADS, GQA_HD)
    rows = S // GRID_W
    row_pos = jnp.broadcast_to(jnp.arange(rows, dtype=f32)[:, None], (rows, GRID_W)).reshape(S)
    col_pos = jnp.broadcast_to(jnp.arange(GRID_W, dtype=f32)[None, :], (rows, GRID_W)).reshape(S)
    half = GQA_HD // 2
    cr, sr = rope_angles(row_pos, half, GQA_ROPE_THETA)
    cc, sc = rope_angles(col_pos, half, GQA_ROPE_THETA)

    def axial(x):
        return jnp.concatenate([apply_rope(x[..., :half], cr, sr),
                                apply_rope(x[..., half:], cc, sc)], axis=-1)

    q = axial(q).reshape(B, S, GQA_KV_HEADS, GQA_GROUP, GQA_HD)
    k = axial(k)
    scale = GQA_HD ** -0.5

    def block(i, qb):
        s = jnp.einsum("bqkgd,bskd->bkgqs", qb, k).astype(f32) * scale
        a = jax.nn.softmax(s, axis=-1).astype(v.dtype)
        return jnp.einsum("bkgqs,bskd->bqkgd", a, v).reshape(B, Q_BLOCK, GQA_WIDTH)

    o = sweep_query_blocks(block, q)
    return (o * jax.nn.silu(gate)) @ p["w_out"]


def diff_mixer(h, p, layer_idx):
    B, S, _ = h.shape
    f32 = jnp.float32
    lambda_init = 0.8 - 0.6 * math.exp(-0.3 * layer_idx)
    proj = h @ p["w_in"]
    q, k, v, gate = jnp.split(
        proj, [DIFF_QK_WIDTH, 2 * DIFF_QK_WIDTH, 2 * DIFF_QK_WIDTH + DIFF_WIDTH], axis=-1)
    q = q.reshape(B, S, DIFF_HEADS, 2, DIFF_HD)
    k = k.reshape(B, S, DIFF_HEADS, 2, DIFF_HD)
    v = v.reshape(B, S, DIFF_HEADS, 2 * DIFF_HD)
    lam = (jnp.exp(jnp.sum(p["lambda_q1"].astype(f32) * p["lambda_k1"].astype(f32)))
           - jnp.exp(jnp.sum(p["lambda_q2"].astype(f32) * p["lambda_k2"].astype(f32)))
           + lambda_init)
    slopes = alibi_slopes(DIFF_HEADS)
    key_pos = jnp.arange(S, dtype=f32)
    scale = DIFF_HD ** -0.5

    def block(i, qb):
        q_pos = (i * Q_BLOCK + jnp.arange(Q_BLOCK)).astype(f32)
        bias = -slopes[:, None, None] * jnp.abs(q_pos[:, None] - key_pos[None, :])[None]
        s = jnp.einsum("bqhcd,bkhcd->bchqk", qb, k).astype(f32) * scale + bias
        a = jax.nn.softmax(s, axis=-1)
        attn = a[:, 0] - lam * a[:, 1]
        o = jnp.einsum("bhqk,bkhd->bqhd", attn.astype(v.dtype), v)
        o = rms_norm(o, p["subln"]) * (1.0 - lambda_init)
        return o.reshape(B, Q_BLOCK, DIFF_WIDTH)

    o = sweep_query_blocks(block, q)
    return (o * jax.nn.silu(gate)) @ p["w_out"]


def setup_inputs(seed: int = 0) -> dict:
    key = jax.random.key(seed)
    k_xp, k_xs, k_layers = jax.random.split(key, 3)
    layer_keys = jax.random.split(k_layers, DEPTH)

    def dense(k, fan_in, fan_out):
        return jax.random.normal(k, (fan_in, fan_out), jnp.float32) * fan_in ** -0.5

    def gain(k, n):
        return 1.0 + 0.02 * jax.random.normal(k, (n,), jnp.float32)

    inputs = {
        "x_prompt": jax.random.normal(k_xp, (BATCH, SEQ, D_MODEL), jnp.float32),
        "x_sample": jax.random.normal(k_xs, (DEC_BATCH, DEC_SEQ, D_MODEL), jnp.float32),
    }
    for i in range(DEPTH):
        ks = jax.random.split(layer_keys[i], 10)
        pre = "l%d_" % i
        kind = i % N_MIXERS
        inputs[pre + "pre_norm"] = gain(ks[0], D_MODEL)
        if kind == 0:
            inputs[pre + "w_in"] = dense(ks[1], D_MODEL, MLA_IN_WIDTH)
            inputs[pre + "q_norm"] = gain(ks[2], MLA_Q_LORA)
            inputs[pre + "w_uq"] = dense(ks[3], MLA_Q_LORA, MLA_HEADS * (MLA_NOPE + MLA_ROPE))
            inputs[pre + "kv_norm"] = gain(ks[4], MLA_KV_LORA)
            inputs[pre + "w_ukv"] = dense(ks[5], MLA_KV_LORA, MLA_HEADS * (MLA_NOPE + MLA_V))
            inputs[pre + "w_out"] = dense(ks[6], MLA_WIDTH, D_MODEL)
        elif kind == 1:
            inputs[pre + "w_in"] = dense(ks[1], D_MODEL, GQA_IN_WIDTH)
            inputs[pre + "q_norm"] = gain(ks[2], GQA_HD)
            inputs[pre + "k_norm"] = gain(ks[3], GQA_HD)
            inputs[pre + "w_out"] = dense(ks[4], GQA_WIDTH, D_MODEL)
        else:
            inputs[pre + "w_in"] = dense(ks[1], D_MODEL, DIFF_IN_WIDTH)
            inputs[pre + "lambda_q1"] = DIFF_LAMBDA_STD * jax.random.normal(ks[2], (DIFF_HD,), jnp.float32)
            inputs[pre + "lambda_k1"] = DIFF_LAMBDA_STD * jax.random.normal(ks[3], (DIFF_HD,), jnp.float32)
            inputs[pre + "lambda_q2"] = DIFF_LAMBDA_STD * jax.random.normal(ks[4], (DIFF_HD,), jnp.float32)
            inputs[pre + "lambda_k2"] = DIFF_LAMBDA_STD * jax.random.normal(ks[5], (DIFF_HD,), jnp.float32)
            inputs[pre + "subln"] = gain(ks[6], 2 * DIFF_HD)
            inputs[pre + "w_out"] = dense(ks[7], DIFF_WIDTH, D_MODEL)
        inputs[pre + "post_norm"] = gain(ks[9], D_MODEL)
    return inputs


def reference(x_prompt, x_sample,
              l0_pre_norm, l0_w_in, l0_q_norm, l0_w_uq, l0_kv_norm, l0_w_ukv, l0_w_out, l0_post_norm,
              l1_pre_norm, l1_w_in, l1_q_norm, l1_k_norm, l1_w_out, l1_post_norm,
              l2_pre_norm, l2_w_in, l2_lambda_q1, l2_lambda_k1, l2_lambda_q2, l2_lambda_k2, l2_subln,
              l2_w_out, l2_post_norm,
              l3_pre_norm, l3_w_in, l3_q_norm, l3_w_uq, l3_kv_norm, l3_w_ukv, l3_w_out, l3_post_norm):
    layers = [
        dict(pre_norm=l0_pre_norm, w_in=l0_w_in, q_norm=l0_q_norm, w_uq=l0_w_uq,
             kv_norm=l0_kv_norm, w_ukv=l0_w_ukv, w_out=l0_w_out, post_norm=l0_post_norm),
        dict(pre_norm=l1_pre_norm, w_in=l1_w_in, q_norm=l1_q_norm, k_norm=l1_k_norm,
             w_out=l1_w_out, post_norm=l1_post_norm),
        dict(pre_norm=l2_pre_norm, w_in=l2_w_in, lambda_q1=l2_lambda_q1, lambda_k1=l2_lambda_k1,
             lambda_q2=l2_lambda_q2, lambda_k2=l2_lambda_k2, subln=l2_subln,
             w_out=l2_w_out, post_norm=l2_post_norm),
        dict(pre_norm=l3_pre_norm, w_in=l3_w_in, q_norm=l3_q_norm, w_uq=l3_w_uq,
             kv_norm=l3_kv_norm, w_ukv=l3_w_ukv, w_out=l3_w_out, post_norm=l3_post_norm),
    ]

    def trunk(x):
        for i in range(DEPTH):
            p = layers[i]
            kind = i % N_MIXERS
            h = rms_norm(x, p["pre_norm"])
            if kind == 0:
                m = mla_mixer(h, p)
            elif kind == 1:
                m = gqa_mixer(h, p)
            else:
                m = diff_mixer(h, p, i)
            x = x + rms_norm(m, p["post_norm"])
        return x

    y_prompt = trunk(x_prompt)
    y_sample = trunk(x_sample)
    return (y_prompt, y_sample)
```

```python
import functools
import math

import jax
import jax.numpy as jnp
from jax import lax
from jax.experimental import pallas as pl
from jax.experimental.pallas import tpu as pltpu

F32 = jnp.float32
BF16 = jnp.bfloat16

D_MODEL = 1024
NORM_EPS = 1e-6
GRID_W = 64

MLA_HEADS = 16
MLA_Q_LORA = 256
MLA_KV_LORA = 128
MLA_NOPE = 64
MLA_ROPE = 32
MLA_V = 64
MLA_QK = MLA_NOPE + MLA_ROPE
MLA_ROPE_THETA = 10000.0

GQA_HEADS = 16
GQA_KV_HEADS = 4
GQA_GROUP = GQA_HEADS // GQA_KV_HEADS
GQA_HD = 64
GQA_ROPE_THETA = 10000.0

DIFF_HEADS = 8
DIFF_HD = 64

LANES = 128
BF16_ROWS = 16
DK = LANES
ONES_ROWS = BF16_ROWS
TOK_TILE = 512
Q_TILE = 256
VMEM_LIMIT_BYTES = 56 * 1024 * 1024


def _rms_rows(x, gain_col):
    ms = jnp.mean(x * x, axis=0, keepdims=True)
    return x * lax.rsqrt(ms + NORM_EPS) * gain_col


def _rope_rows(x1, x2, cos, sin):
    return x1 * cos - x2 * sin, x1 * sin + x2 * cos


def _dot(a, b):
    return jnp.dot(a, b, preferred_element_type=F32)


def _mla_pre_kernel(x_ref, g_ref, win_ref, qn_ref, wuq_ref, kvn_ref, wv_ref, wk_ref,
                    cos_ref, sin_ref, q_out, k_out, v_out, gate_out):
    t = x_ref.shape[2]
    h = _rms_rows(x_ref[0], g_ref[...]).astype(BF16)
    cos = cos_ref[...]
    sin = sin_ref[...]
    scale = MLA_QK ** -0.5

    cq = _dot(win_ref[0:MLA_Q_LORA, :], h)
    cqn = _rms_rows(cq, qn_ref[...]).astype(BF16)
    q_all = _dot(wuq_ref[...], cqn) * scale
    zeros_q = jnp.zeros((DK - MLA_QK, t), F32)
    half = MLA_ROPE // 2
    for hd in range(MLA_HEADS):
        base = hd * MLA_QK
        nope = q_all[base:base + MLA_NOPE]
        r1, r2 = _rope_rows(q_all[base + MLA_NOPE:base + MLA_NOPE + half],
                            q_all[base + MLA_NOPE + half:base + MLA_QK], cos, sin)
        q_out[0, hd] = jnp.concatenate([nope, r1, r2, zeros_q], axis=0).astype(BF16)

    kv_lo = MLA_Q_LORA
    kv_hi = MLA_Q_LORA + MLA_KV_LORA + MLA_ROPE
    ckr = _dot(win_ref[kv_lo:kv_hi, :], h)
    ckvn = _rms_rows(ckr[0:MLA_KV_LORA], kvn_ref[...])
    v_all = _dot(wv_ref[...], ckvn.astype(BF16))
    ones = jnp.ones((ONES_ROWS, t), BF16)
    for hd in range(MLA_HEADS):
        v_out[0, hd, 0, 0:MLA_V, :] = v_all[hd * MLA_V:(hd + 1) * MLA_V].astype(BF16)
        v_out[0, hd, 0, MLA_V:MLA_V + ONES_ROWS, :] = ones

    kr = ckr[MLA_KV_LORA:MLA_KV_LORA + MLA_ROPE]
    r1, r2 = _rope_rows(kr[0:half], kr[half:MLA_ROPE], cos, sin)
    krr = jnp.concatenate([r1, r2, jnp.zeros((LANES - MLA_ROPE, t), F32)], axis=0)
    z = jnp.concatenate([ckvn.T, krr.T], axis=1).astype(BF16)
    k_all = _dot(z, wk_ref[...])
    for hd in range(MLA_HEADS):
        k_out[0, hd] = k_all[:, hd * DK:(hd + 1) * DK].astype(BF16)

    gate_out[0] = _dot(win_ref[kv_hi:kv_hi + D_MODEL, :], h).astype(BF16)


def _gqa_pre_kernel(x_ref, g_ref, win_ref, qn_ref, kn_ref, cr_ref, sr_ref, cc_ref, sc_ref,
                    q_out, k_out, v_out, gate_out):
    t = x_ref.shape[2]
    h = _rms_rows(x_ref[0], g_ref[...]).astype(BF16)
    cr, sr, cc, sc = cr_ref[...], sr_ref[...], cc_ref[...], sc_ref[...]
    scale = GQA_HD ** -0.5
    n_q = GQA_HEADS * GQA_HD
    n_kv = GQA_KV_HEADS * GQA_HD
    qk = _dot(win_ref[0:n_q + n_kv, :], h)
    zeros_q = jnp.zeros((DK - GQA_HD, t), F32)
    quarter = GQA_HD // 4

    def norm_rope(blk, gain_col):
        xn = _rms_rows(blk, gain_col)
        a, b = _rope_rows(xn[0:quarter], xn[quarter:2 * quarter], cr, sr)
        c, d = _rope_rows(xn[2 * quarter:3 * quarter], xn[3 * quarter:4 * quarter], cc, sc)
        return jnp.concatenate([a, b, c, d], axis=0)

    for hd in range(GQA_HEADS):
        rq = norm_rope(qk[hd * GQA_HD:(hd + 1) * GQA_HD], qn_ref[...]) * scale
        kv = hd // GQA_GROUP
        parts = [rq, zeros_q] if kv % 2 == 0 else [zeros_q, rq]
        q_out[0, hd] = jnp.concatenate(parts, axis=0).astype(BF16)

    k_rows = [norm_rope(qk[n_q + kv * GQA_HD:n_q + (kv + 1) * GQA_HD], kn_ref[...])
              for kv in range(GQA_KV_HEADS)]
    k_nat = jnp.concatenate(k_rows, axis=0).T
    for j in range(n_kv // DK):
        k_out[0, j] = k_nat[:, j * DK:(j + 1) * DK].astype(BF16)

    v = _dot(win_ref[n_q + n_kv:n_q + 2 * n_kv, :], h)
    ones = jnp.ones((ONES_ROWS, t), BF16)
    for kv in range(GQA_KV_HEADS):
        v_out[0, kv, 0, 0:GQA_HD, :] = v[kv * GQA_HD:(kv + 1) * GQA_HD].astype(BF16)
        v_out[0, kv, 0, GQA_HD:GQA_HD + ONES_ROWS, :] = ones

    g_lo = n_q + 2 * n_kv
    gate_out[0] = _dot(win_ref[g_lo:g_lo + D_MODEL, :], h).astype(BF16)


def _diff_pre_kernel(x_ref, g_ref, win_ref, wk_ref, q_out, k_out, v_out, gate_out):
    t = x_ref.shape[2]
    h = _rms_rows(x_ref[0], g_ref[...]).astype(BF16)
    scale = DIFF_HD ** -0.5
    width = DIFF_HEADS * 2 * DIFF_HD
    q = _dot(win_ref[0:width, :], h) * scale
    zeros_q = jnp.zeros((DK - DIFF_HD, t), F32)
    for hc in range(2 * DIFF_HEADS):
        blk = q[hc * DIFF_HD:(hc + 1) * DIFF_HD]
        parts = [blk, zeros_q] if hc % 2 == 0 else [zeros_q, blk]
        q_out[0, hc] = jnp.concatenate(parts, axis=0).astype(BF16)

    k_nat = lax.dot_general(h, wk_ref[...], (((0,), (0,)), ((), ())),
                            preferred_element_type=F32)
    for hd in range(DIFF_HEADS):
        k_out[0, hd] = k_nat[:, hd * DK:(hd + 1) * DK].astype(BF16)

    v = _dot(win_ref[width:2 * width, :], h)
    dv = 2 * DIFF_HD
    ones = jnp.ones((ONES_ROWS, t), BF16)
    for hd in range(DIFF_HEADS):
        v_out[0, hd, 0, 0:dv, :] = v[hd * dv:(hd + 1) * dv].astype(BF16)
        v_out[0, hd, 0, dv:dv + ONES_ROWS, :] = ones

    gate_out[0] = _dot(win_ref[2 * width:3 * width, :], h).astype(BF16)


def _attend(q_t, k_ref, v_ref, dv, bias_fn=None):
    tq = q_t.shape[1]
    nkc, dvx, tk = v_ref.shape[2], v_ref.shape[3], v_ref.shape[4]

    def body(kc, carry):
        m, acc = carry
        start = pl.multiple_of(kc * tk, tk)
        s = _dot(k_ref[0, 0, pl.ds(start, tk), :], q_t)
        if bias_fn is not None:
            s = s + bias_fn(kc)
        m_new = jnp.maximum(m, jnp.max(s, axis=0, keepdims=True))
        alpha = jnp.exp(m - m_new)
        p = jnp.exp(s - m_new).astype(BF16)
        acc = alpha * acc + _dot(v_ref[0, 0, kc], p)
        return m_new, acc

    m0 = jnp.full((1, tq), -jnp.inf, F32)
    acc0 = jnp.zeros((dvx, tq), F32)
    _, acc = lax.fori_loop(0, nkc, body, (m0, acc0))
    return acc[0:dv] / acc[dv:dv + 1]


def _mla_attn_kernel(q_ref, k_ref, v_ref, o_ref):
    o_ref[0] = _attend(q_ref[0, 0], k_ref, v_ref, MLA_V).astype(BF16)


def _gqa_attn_kernel(q_ref, k_ref, v_ref, o_ref):
    for g in range(GQA_GROUP):
        o = _attend(q_ref[0, g], k_ref, v_ref, GQA_HD)
        o_ref[0, g * GQA_HD:(g + 1) * GQA_HD, :] = o.astype(BF16)


def _diff_attn_kernel(q_ref, k_ref, v_ref, lq1_ref, lk1_ref, lq2_ref, lk2_ref, subln_ref,
                      o_ref, *, lambda_init):
    tq = q_ref.shape[3]
    tk = v_ref.shape[4]
    lam = (jnp.exp(jnp.sum(lq1_ref[...] * lk1_ref[...], axis=1, keepdims=True))
           - jnp.exp(jnp.sum(lq2_ref[...] * lk2_ref[...], axis=1, keepdims=True))
           + lambda_init)
    exponent = jnp.full((1, 1), 126 - pl.program_id(1), jnp.int32)
    slope = lax.bitcast_convert_type(exponent << 23, F32)
    q_pos = (pl.program_id(2) * tq + lax.broadcasted_iota(jnp.int32, (1, tq), 1)).astype(F32)
    k_iota = lax.broadcasted_iota(jnp.int32, (tk, 1), 0)

    def bias_fn(kc):
        k_pos = (kc * tk + k_iota).astype(F32)
        return -slope * jnp.abs(q_pos - k_pos)

    dv = 2 * DIFF_HD
    o0 = _attend(q_ref[0, 0], k_ref, v_ref, dv, bias_fn)
    o1 = _attend(q_ref[0, 1], k_ref, v_ref, dv, bias_fn)
    o = o0 - lam * o1
    o_ref[0] = (_rms_rows(o, subln_ref[...]) * (1.0 - lambda_init)).astype(BF16)


def _post_kernel(o_ref, gate_ref, x_ref, wout_ref, g_ref, y_ref):
    gate = gate_ref[0].astype(F32)
    og = (o_ref[0].astype(F32) * (gate / (1.0 + jnp.exp(-gate)))).astype(BF16)
    m = _dot(wout_ref[...], og)
    y_ref[0] = x_ref[0] + _rms_rows(m, g_ref[...])


def _params(n_axes):
    return pltpu.CompilerParams(dimension_semantics=("arbitrary",) * n_axes,
                                vmem_limit_bytes=VMEM_LIMIT_BYTES)


def _const_spec(arr):
    nd = arr.ndim
    return pl.BlockSpec(arr.shape, lambda *_: (0,) * nd)


def _col(v):
    return v.astype(F32).reshape(-1, 1)


def _tiles(s):
    tok = min(TOK_TILE, s)
    tq = min(Q_TILE, s)
    assert s % tok == 0 and s % tq == 0
    return tok, tq


def _pre_call(kernel, name, x_t, consts, tables, n_heads, n_kblocks, n_vheads, dv):
    b, _, s = x_t.shape
    tok, _ = _tiles(s)
    dvx = dv + ONES_ROWS
    in_specs = ([pl.BlockSpec((1, D_MODEL, tok), lambda i, t: (i, 0, t))]
                + [_const_spec(c) for c in consts]
                + [pl.BlockSpec((tb.shape[0], tok), lambda i, t: (0, t)) for tb in tables])
    out_shape = (jax.ShapeDtypeStruct((b, n_heads, DK, s), BF16),
                 jax.ShapeDtypeStruct((b, n_kblocks, s, DK), BF16),
                 jax.ShapeDtypeStruct((b, n_vheads, s // tok, dvx, tok), BF16),
                 jax.ShapeDtypeStruct((b, D_MODEL, s), BF16))
    out_specs = (pl.BlockSpec((1, n_heads, DK, tok), lambda i, t: (i, 0, 0, t)),
                 pl.BlockSpec((1, n_kblocks, tok, DK), lambda i, t: (i, 0, t, 0)),
                 pl.BlockSpec((1, n_vheads, 1, dvx, tok), lambda i, t: (i, 0, t, 0, 0)),
                 pl.BlockSpec((1, D_MODEL, tok), lambda i, t: (i, 0, t)))
    return pl.pallas_call(kernel, out_shape=out_shape, grid=(b, s // tok), in_specs=in_specs,
                          out_specs=out_specs, compiler_params=_params(2), name=name)(
                              x_t, *consts, *tables)


def _attn_call(kernel, name, q_t, k, v_t, extra, group, k_block_of_group, out_rows):
    b, n_heads, _, s = q_t.shape
    _, tq = _tiles(s)
    _, n_vheads, nkc, dvx, tk = v_t.shape
    n_groups = n_heads // group
    assert n_groups == n_vheads
    in_specs = ([pl.BlockSpec((1, group, DK, tq), lambda i, g, qi: (i, g, 0, qi)),
                 pl.BlockSpec((1, 1, s, DK), lambda i, g, qi: (i, k_block_of_group(g), 0, 0)),
                 pl.BlockSpec((1, 1, nkc, dvx, tk), lambda i, g, qi: (i, g, 0, 0, 0))]
                + [_const_spec(e) for e in extra])
    return pl.pallas_call(
        kernel, out_shape=jax.ShapeDtypeStruct((b, D_MODEL, s), BF16),
        grid=(b, n_groups, s // tq), in_specs=in_specs,
        out_specs=pl.BlockSpec((1, out_rows, tq), lambda i, g, qi: (i, g, qi)),
        compiler_params=_params(3), name=name)(q_t, k, v_t, *extra)


def _post_call(o_t, gate_t, x_t, wout_t, post_gain):
    b, _, s = x_t.shape
    tok, _ = _tiles(s)
    tile = pl.BlockSpec((1, D_MODEL, tok), lambda i, t: (i, 0, t))
    consts = (wout_t, _col(post_gain))
    return pl.pallas_call(
        _post_kernel, out_shape=jax.ShapeDtypeStruct((b, D_MODEL, s), F32),
        grid=(b, s // tok), in_specs=[tile, tile, tile] + [_const_spec(c) for c in consts],
        out_specs=tile, compiler_params=_params(2), name="post")(o_t, gate_t, x_t, *consts)


def _rope_tables(pos, dim, theta):
    inv = 1.0 / (theta ** (jnp.arange(0, dim, 2, dtype=F32) / dim))
    ang = pos[:, None] * inv[None, :]
    return jnp.cos(ang).T, jnp.sin(ang).T


def _mla_layer(x_t, p):
    s = x_t.shape[2]
    w_ukv = p["w_ukv"].reshape(MLA_KV_LORA, MLA_HEADS, MLA_NOPE + MLA_V)
    wv_t = w_ukv[:, :, MLA_NOPE:].reshape(MLA_KV_LORA, MLA_HEADS * MLA_V).T
    wk = jnp.zeros((2 * LANES, MLA_HEADS, DK), F32)
    wk = wk.at[0:MLA_KV_LORA, :, 0:MLA_NOPE].set(w_ukv[:, :, 0:MLA_NOPE])
    eye = jnp.broadcast_to(jnp.eye(MLA_ROPE, dtype=F32)[:, None, :], (MLA_ROPE, MLA_HEADS, MLA_ROPE))
    wk = wk.at[LANES:LANES + MLA_ROPE, :, MLA_NOPE:MLA_QK].set(eye)
    consts = (_col(p["pre_norm"]), p["w_in"].T.astype(BF16), _col(p["q_norm"]),
              p["w_uq"].T.astype(BF16), _col(p["kv_norm"]), wv_t.astype(BF16),
              wk.reshape(2 * LANES, MLA_HEADS * DK).astype(BF16))
    tables = _rope_tables(jnp.arange(s, dtype=F32), MLA_ROPE, MLA_ROPE_THETA)
    q_t, k, v_t, gate_t = _pre_call(_mla_pre_kernel, "mla_pre", x_t, consts, tables,
                                    MLA_HEADS, MLA_HEADS, MLA_HEADS, MLA_V)
    o_t = _attn_call(_mla_attn_kernel, "mla_attn", q_t, k, v_t, (), 1, lambda g: g, MLA_V)
    return _post_call(o_t, gate_t, x_t, p["w_out"].T.astype(BF16), p["post_norm"])


def _gqa_layer(x_t, p):
    s = x_t.shape[2]
    consts = (_col(p["pre_norm"]), p["w_in"].T.astype(BF16), _col(p["q_norm"]), _col(p["k_norm"]))
    t_idx = jnp.arange(s)
    half = GQA_HD // 2
    tables = (_rope_tables((t_idx // GRID_W).astype(F32), half, GQA_ROPE_THETA)
              + _rope_tables((t_idx % GRID_W).astype(F32), half, GQA_ROPE_THETA))
    q_t, k, v_t, gate_t = _pre_call(_gqa_pre_kernel, "gqa_pre", x_t, consts, tables,
                                    GQA_HEADS, GQA_KV_HEADS * GQA_HD // DK, GQA_KV_HEADS, GQA_HD)
    o_t = _attn_call(_gqa_attn_kernel, "gqa_attn", q_t, k, v_t, (), GQA_GROUP,
                     lambda g: g // 2, GQA_GROUP * GQA_HD)
    return _post_call(o_t, gate_t, x_t, p["w_out"].T.astype(BF16), p["post_norm"])


def _diff_layer(x_t, p, layer_idx):
    lambda_init = 0.8 - 0.6 * math.exp(-0.3 * layer_idx)
    width = DIFF_HEADS * 2 * DIFF_HD
    w_in = p["w_in"]
    w_qvg = jnp.concatenate([w_in[:, 0:width], w_in[:, 2 * width:4 * width]], axis=1)
    consts = (_col(p["pre_norm"]), w_qvg.T.astype(BF16), w_in[:, width:2 * width].astype(BF16))
    q_t, k, v_t, gate_t = _pre_call(_diff_pre_kernel, "diff_pre", x_t, consts, (),
                                    2 * DIFF_HEADS, DIFF_HEADS, DIFF_HEADS, 2 * DIFF_HD)
    row = lambda v: v.astype(F32).reshape(1, -1)
    extra = (row(p["lambda_q1"]), row(p["lambda_k1"]), row(p["lambda_q2"]), row(p["lambda_k2"]),
             _col(p["subln"]))
    o_t = _attn_call(functools.partial(_diff_attn_kernel, lambda_init=lambda_init), "diff_attn",
                     q_t, k, v_t, extra, 2, lambda g: g, 2 * DIFF_HD)
    return _post_call(o_t, gate_t, x_t, p["w_out"].T.astype(BF16), p["post_norm"])


def _trunk(x, layers):
    x_t = jnp.swapaxes(x, 1, 2)
    for i, p in enumerate(layers):
        kind = i % 3
        if kind == 0:
            x_t = _mla_layer(x_t, p)
        elif kind == 1:
            x_t = _gqa_layer(x_t, p)
        else:
            x_t = _diff_layer(x_t, p, i)
    return jnp.swapaxes(x_t, 1, 2)


def kernel(x_prompt, x_sample, l0_pre_norm, l0_w_in, l0_q_norm, l0_w_uq, l0_kv_norm, l0_w_ukv, l0_w_out, l0_post_norm, l1_pre_norm, l1_w_in, l1_q_norm, l1_k_norm, l1_w_out, l1_post_norm, l2_pre_norm, l2_w_in, l2_lambda_q1, l2_lambda_k1, l2_lambda_q2, l2_lambda_k2, l2_subln, l2_w_out, l2_post_norm, l3_pre_norm, l3_w_in, l3_q_norm, l3_w_uq, l3_kv_norm, l3_w_ukv, l3_w_out, l3_post_norm):
    layers = [
        dict(pre_norm=l0_pre_norm, w_in=l0_w_in, q_norm=l0_q_norm, w_uq=l0_w_uq,
             kv_norm=l0_kv_norm, w_ukv=l0_w_ukv, w_out=l0_w_out, post_norm=l0_post_norm),
        dict(pre_norm=l1_pre_norm, w_in=l1_w_in, q_norm=l1_q_norm, k_norm=l1_k_norm,
             w_out=l1_w_out, post_norm=l1_post_norm),
        dict(pre_norm=l2_pre_norm, w_in=l2_w_in, lambda_q1=l2_lambda_q1, lambda_k1=l2_lambda_k1,
             lambda_q2=l2_lambda_q2, lambda_k2=l2_lambda_k2, subln=l2_subln,
             w_out=l2_w_out, post_norm=l2_post_norm),
        dict(pre_norm=l3_pre_norm, w_in=l3_w_in, q_norm=l3_q_norm, w_uq=l3_w_uq,
             kv_norm=l3_kv_norm, w_ukv=l3_w_ukv, w_out=l3_w_out, post_norm=l3_post_norm),
    ]
    return (_trunk(x_prompt, layers), _trunk(x_sample, layers))
```

```python
import functools
import math

import jax
import jax.numpy as jnp
from jax import lax
from jax.experimental import pallas as pl
from jax.experimental.pallas import tpu as pltpu

F32 = jnp.float32
BF16 = jnp.bfloat16

D_MODEL = 1024
NORM_EPS = 1e-6
GRID_W = 64

MLA_HEADS = 16
MLA_Q_LORA = 256
MLA_KV_LORA = 128
MLA_NOPE = 64
MLA_ROPE = 32
MLA_V = 64
MLA_QK = MLA_NOPE + MLA_ROPE
MLA_ROPE_THETA = 10000.0

GQA_HEADS = 16
GQA_KV_HEADS = 4
GQA_GROUP = GQA_HEADS // GQA_KV_HEADS
GQA_HD = 64
GQA_ROPE_THETA = 10000.0

DIFF_HEADS = 8
DIFF_HD = 64

LOG2_E = math.log2(math.e)

LANES = 128
BF16_ROWS = 16
DK = LANES
ONES_ROWS = BF16_ROWS
TOK_TILE = 512
Q_TILE = 256
ATTN_STREAMS = 4
VMEM_LIMIT_BYTES = 56 * 1024 * 1024


def _rms_rows(x, gain_col):
    ms = jnp.mean(x * x, axis=0, keepdims=True)
    return x * lax.rsqrt(ms + NORM_EPS) * gain_col


def _rope_rows(x1, x2, cos, sin):
    return x1 * cos - x2 * sin, x1 * sin + x2 * cos


def _dot(a, b):
    return jnp.dot(a, b, preferred_element_type=F32)


def _mla_pre_kernel(x_ref, g_ref, win_ref, qn_ref, wuq_ref, kvn_ref, wv_ref, wk_ref,
                    cos_ref, sin_ref, q_out, k_out, v_out, gate_out):
    t = x_ref.shape[2]
    h = _rms_rows(x_ref[0], g_ref[...]).astype(BF16)
    cos = cos_ref[...]
    sin = sin_ref[...]
    scale = MLA_QK ** -0.5 * LOG2_E

    cq = _dot(win_ref[0:MLA_Q_LORA, :], h)
    cqn = _rms_rows(cq, qn_ref[...]).astype(BF16)
    q_all = _dot(wuq_ref[...], cqn) * scale
    zeros_q = jnp.zeros((DK - MLA_QK, t), F32)
    half = MLA_ROPE // 2
    for hd in range(MLA_HEADS):
        base = hd * MLA_QK
        nope = q_all[base:base + MLA_NOPE]
        r1, r2 = _rope_rows(q_all[base + MLA_NOPE:base + MLA_NOPE + half],
                            q_all[base + MLA_NOPE + half:base + MLA_QK], cos, sin)
        q_out[0, hd] = jnp.concatenate([nope, r1, r2, zeros_q], axis=0).astype(BF16)

    kv_lo = MLA_Q_LORA
    kv_hi = MLA_Q_LORA + MLA_KV_LORA + MLA_ROPE
    ckr = _dot(win_ref[kv_lo:kv_hi, :], h)
    ckvn = _rms_rows(ckr[0:MLA_KV_LORA], kvn_ref[...])
    v_all = _dot(wv_ref[...], ckvn.astype(BF16))
    ones = jnp.ones((ONES_ROWS, t), BF16)
    for hd in range(MLA_HEADS):
        v_out[0, hd, 0, 0:MLA_V, :] = v_all[hd * MLA_V:(hd + 1) * MLA_V].astype(BF16)
        v_out[0, hd, 0, MLA_V:MLA_V + ONES_ROWS, :] = ones

    kr = ckr[MLA_KV_LORA:MLA_KV_LORA + MLA_ROPE]
    r1, r2 = _rope_rows(kr[0:half], kr[half:MLA_ROPE], cos, sin)
    krr = jnp.concatenate([r1, r2, jnp.zeros((LANES - MLA_ROPE, t), F32)], axis=0)
    z = jnp.concatenate([ckvn.T, krr.T], axis=1).astype(BF16)
    k_all = _dot(z, wk_ref[...])
    for hd in range(MLA_HEADS):
        k_out[0, hd] = k_all[:, hd * DK:(hd + 1) * DK].astype(BF16)

    gate_out[0] = _dot(win_ref[kv_hi:kv_hi + D_MODEL, :], h).astype(BF16)


def _gqa_pre_kernel(x_ref, g_ref, win_ref, qn_ref, kn_ref, cr_ref, sr_ref, cc_ref, sc_ref,
                    q_out, k_out, v_out, gate_out):
    t = x_ref.shape[2]
    h = _rms_rows(x_ref[0], g_ref[...]).astype(BF16)
    cr, sr, cc, sc = cr_ref[...], sr_ref[...], cc_ref[...], sc_ref[...]
    scale = GQA_HD ** -0.5 * LOG2_E
    n_q = GQA_HEADS * GQA_HD
    n_kv = GQA_KV_HEADS * GQA_HD
    qk = _dot(win_ref[0:n_q + n_kv, :], h)
    zeros_q = jnp.zeros((DK - GQA_HD, t), F32)
    quarter = GQA_HD // 4

    def norm_rope(blk, gain_col):
        xn = _rms_rows(blk, gain_col)
        a, b = _rope_rows(xn[0:quarter], xn[quarter:2 * quarter], cr, sr)
        c, d = _rope_rows(xn[2 * quarter:3 * quarter], xn[3 * quarter:4 * quarter], cc, sc)
        return jnp.concatenate([a, b, c, d], axis=0)

    for hd in range(GQA_HEADS):
        rq = norm_rope(qk[hd * GQA_HD:(hd + 1) * GQA_HD], qn_ref[...]) * scale
        kv = hd // GQA_GROUP
        parts = [rq, zeros_q] if kv % 2 == 0 else [zeros_q, rq]
        q_out[0, hd] = jnp.concatenate(parts, axis=0).astype(BF16)

    k_rows = [norm_rope(qk[n_q + kv * GQA_HD:n_q + (kv + 1) * GQA_HD], kn_ref[...])
              for kv in range(GQA_KV_HEADS)]
    k_nat = jnp.concatenate(k_rows, axis=0).T
    for j in range(n_kv // DK):
        k_out[0, j] = k_nat[:, j * DK:(j + 1) * DK].astype(BF16)

    v = _dot(win_ref[n_q + n_kv:n_q + 2 * n_kv, :], h)
    ones = jnp.ones((ONES_ROWS, t), BF16)
    for kv in range(GQA_KV_HEADS):
        v_out[0, kv, 0, 0:GQA_HD, :] = v[kv * GQA_HD:(kv + 1) * GQA_HD].astype(BF16)
        v_out[0, kv, 0, GQA_HD:GQA_HD + ONES_ROWS, :] = ones

    g_lo = n_q + 2 * n_kv
    gate_out[0] = _dot(win_ref[g_lo:g_lo + D_MODEL, :], h).astype(BF16)


def _diff_pre_kernel(x_ref, g_ref, win_ref, wk_ref, q_out, k_out, v_out, gate_out):
    t = x_ref.shape[2]
    h = _rms_rows(x_ref[0], g_ref[...]).astype(BF16)
    scale = DIFF_HD ** -0.5 * LOG2_E
    width = DIFF_HEADS * 2 * DIFF_HD
    q = _dot(win_ref[0:width, :], h) * scale
    zeros_q = jnp.zeros((DK - DIFF_HD, t), F32)
    for hc in range(2 * DIFF_HEADS):
        blk = q[hc * DIFF_HD:(hc + 1) * DIFF_HD]
        parts = [blk, zeros_q] if hc % 2 == 0 else [zeros_q, blk]
        q_out[0, hc] = jnp.concatenate(parts, axis=0).astype(BF16)

    k_nat = lax.dot_general(h, wk_ref[...], (((0,), (0,)), ((), ())),
                            preferred_element_type=F32)
    for hd in range(DIFF_HEADS):
        k_out[0, hd] = k_nat[:, hd * DK:(hd + 1) * DK].astype(BF16)

    v = _dot(win_ref[width:2 * width, :], h)
    dv = 2 * DIFF_HD
    ones = jnp.ones((ONES_ROWS, t), BF16)
    for hd in range(DIFF_HEADS):
        v_out[0, hd, 0, 0:dv, :] = v[hd * dv:(hd + 1) * dv].astype(BF16)
        v_out[0, hd, 0, dv:dv + ONES_ROWS, :] = ones

    gate_out[0] = _dot(win_ref[2 * width:3 * width, :], h).astype(BF16)


def _attend(q_ts, k_chunk, v_chunk, nkc, dv, scratch, bias_fn=None):
    n = len(q_ts)
    tq = q_ts[0].shape[1]
    s_slots, p_slots = scratch[0:2], scratch[2:4]
    dvx = dv + ONES_ROWS
    assert nkc >= 2 and nkc % 2 == 0

    def qk(c, s_ref):
        maxima = []
        for g in range(n):
            s = _dot(k_chunk(g, c), q_ts[g])
            if bias_fn is not None:
                s = s + bias_fn(g, c)
            s_ref[g] = s
            maxima.append(jnp.max(s, axis=0, keepdims=True))
        return tuple(maxima)

    def softmax(s_ref, s_max, p_ref, m):
        m_out, alpha = [], []
        for g in range(n):
            m_new = jnp.maximum(m[g], s_max[g])
            alpha.append(jnp.exp2(m[g] - m_new))
            p_ref[g] = jnp.exp2(s_ref[g] - m_new).astype(BF16)
            m_out.append(m_new)
        return tuple(m_out), tuple(alpha)

    def pv(c, p_ref, alpha, acc):
        return tuple(alpha[g] * acc[g] + _dot(v_chunk(g, c), p_ref[g]) for g in range(n))

    m = tuple(jnp.full((1, tq), -jnp.inf, F32) for _ in range(n))
    acc = tuple(jnp.zeros((dvx, tq), F32) for _ in range(n))
    max0 = qk(0, s_slots[0])
    max1 = qk(1, s_slots[1])
    m, alpha = softmax(s_slots[0], max0, p_slots[0], m)

    def pair(j, carry):
        m, alpha, acc, max1 = carry
        t = 2 * j + 1
        max0 = qk(t + 1, s_slots[0])
        acc = pv(t - 1, p_slots[0], alpha, acc)
        m, alpha = softmax(s_slots[1], max1, p_slots[1], m)
        max1 = qk(t + 2, s_slots[1])
        acc = pv(t, p_slots[1], alpha, acc)
        m, alpha = softmax(s_slots[0], max0, p_slots[0], m)
        return m, alpha, acc, max1

    m, alpha, acc, max1 = lax.fori_loop(0, (nkc - 2) // 2, pair, (m, alpha, acc, max1))
    acc = pv(nkc - 2, p_slots[0], alpha, acc)
    m, alpha = softmax(s_slots[1], max1, p_slots[1], m)
    acc = pv(nkc - 1, p_slots[1], alpha, acc)
    return [a[0:dv] / a[dv:dv + 1] for a in acc]


def _key_rows(c, tk):
    return pl.ds(pl.multiple_of(c * tk, tk), tk)


def _mla_attn_kernel(q_ref, k_ref, v_ref, o_ref, *scratch):
    nkc, tk = v_ref.shape[2], v_ref.shape[4]
    outs = _attend([q_ref[0, g] for g in range(ATTN_STREAMS)],
                   lambda g, c: k_ref[0, g, _key_rows(c, tk), :],
                   lambda g, c: v_ref[0, g, c], nkc, MLA_V, scratch)
    for g, o in enumerate(outs):
        o_ref[0, g * MLA_V:(g + 1) * MLA_V, :] = o.astype(BF16)


def _gqa_attn_kernel(q_ref, k_ref, v_ref, o_ref, *scratch):
    nkc, tk = v_ref.shape[2], v_ref.shape[4]
    outs = _attend([q_ref[0, g] for g in range(ATTN_STREAMS)],
                   lambda g, c: k_ref[0, 0, _key_rows(c, tk), :],
                   lambda g, c: v_ref[0, 0, c], nkc, GQA_HD, scratch)
    for g, o in enumerate(outs):
        o_ref[0, g * GQA_HD:(g + 1) * GQA_HD, :] = o.astype(BF16)


def _diff_attn_kernel(q_ref, k_ref, v_ref, lq1_ref, lk1_ref, lq2_ref, lk2_ref, subln_ref,
                      o_ref, *scratch, lambda_init):
    tq = q_ref.shape[3]
    nkc, tk = v_ref.shape[2], v_ref.shape[4]
    heads = ATTN_STREAMS // 2
    lam = (jnp.exp(jnp.sum(lq1_ref[...] * lk1_ref[...], axis=1, keepdims=True))
           - jnp.exp(jnp.sum(lq2_ref[...] * lk2_ref[...], axis=1, keepdims=True))
           + lambda_init)
    q_pos = (pl.program_id(2) * tq + lax.broadcasted_iota(jnp.int32, (1, tq), 1)).astype(F32)
    k_iota = lax.broadcasted_iota(jnp.int32, (tk, 1), 0)
    slopes = []
    for hd in range(heads):
        exponent = jnp.full((1, 1), 126 - (pl.program_id(1) * heads + hd), jnp.int32)
        slopes.append(lax.bitcast_convert_type(exponent << 23, F32) * LOG2_E)

    def bias_fn(g, c):
        k_pos = (c * tk + k_iota).astype(F32)
        return -slopes[g // 2] * jnp.abs(q_pos - k_pos)

    dv = 2 * DIFF_HD
    outs = _attend([q_ref[0, g] for g in range(ATTN_STREAMS)],
                   lambda g, c: k_ref[0, g // 2, _key_rows(c, tk), :],
                   lambda g, c: v_ref[0, g // 2, c], nkc, dv, scratch, bias_fn)
    for hd in range(heads):
        o = outs[2 * hd] - lam * outs[2 * hd + 1]
        o = _rms_rows(o, subln_ref[...]) * (1.0 - lambda_init)
        o_ref[0, hd * dv:(hd + 1) * dv, :] = o.astype(BF16)


def _post_kernel(o_ref, gate_ref, x_ref, wout_ref, g_ref, y_ref):
    gate = gate_ref[0].astype(F32)
    og = (o_ref[0].astype(F32) * (gate / (1.0 + jnp.exp(-gate)))).astype(BF16)
    m = _dot(wout_ref[...], og)
    y_ref[0] = x_ref[0] + _rms_rows(m, g_ref[...])


def _params(n_axes):
    return pltpu.CompilerParams(dimension_semantics=("arbitrary",) * n_axes,
                                vmem_limit_bytes=VMEM_LIMIT_BYTES)


def _const_spec(arr):
    nd = arr.ndim
    return pl.BlockSpec(arr.shape, lambda *_: (0,) * nd)


def _col(v):
    return v.astype(F32).reshape(-1, 1)


def _tiles(s):
    tok = min(TOK_TILE, s)
    tq = min(Q_TILE, s)
    assert s % tok == 0 and s % tq == 0
    return tok, tq


def _pre_call(kernel, name, x_t, consts, tables, n_heads, n_kblocks, n_vheads, dv):
    b, _, s = x_t.shape
    tok, _ = _tiles(s)
    dvx = dv + ONES_ROWS
    in_specs = ([pl.BlockSpec((1, D_MODEL, tok), lambda i, t: (i, 0, t))]
                + [_const_spec(c) for c in consts]
                + [pl.BlockSpec((tb.shape[0], tok), lambda i, t: (0, t)) for tb in tables])
    out_shape = (jax.ShapeDtypeStruct((b, n_heads, DK, s), BF16),
                 jax.ShapeDtypeStruct((b, n_kblocks, s, DK), BF16),
                 jax.ShapeDtypeStruct((b, n_vheads, s // tok, dvx, tok), BF16),
                 jax.ShapeDtypeStruct((b, D_MODEL, s), BF16))
    out_specs = (pl.BlockSpec((1, n_heads, DK, tok), lambda i, t: (i, 0, 0, t)),
                 pl.BlockSpec((1, n_kblocks, tok, DK), lambda i, t: (i, 0, t, 0)),
                 pl.BlockSpec((1, n_vheads, 1, dvx, tok), lambda i, t: (i, 0, t, 0, 0)),
                 pl.BlockSpec((1, D_MODEL, tok), lambda i, t: (i, 0, t)))
    return pl.pallas_call(kernel, out_shape=out_shape, grid=(b, s // tok), in_specs=in_specs,
                          out_specs=out_specs, compiler_params=_params(2), name=name)(
                              x_t, *consts, *tables)


def _attn_call(kernel, name, q_t, k, v_t, extra, k_blocks, k_block_of_group, v_heads):
    b, n_heads, _, s = q_t.shape
    _, tq = _tiles(s)
    nkc, dvx, tk = v_t.shape[2:]
    n_groups = n_heads // ATTN_STREAMS
    assert n_groups * v_heads == v_t.shape[1]
    in_specs = ([pl.BlockSpec((1, ATTN_STREAMS, DK, tq), lambda i, g, qi: (i, g, 0, qi)),
                 pl.BlockSpec((1, k_blocks, s, DK),
                              lambda i, g, qi: (i, k_block_of_group(g), 0, 0)),
                 pl.BlockSpec((1, v_heads, nkc, dvx, tk), lambda i, g, qi: (i, g, 0, 0, 0))]
                + [_const_spec(e) for e in extra])
    scratch = ([pltpu.VMEM((ATTN_STREAMS, tk, tq), F32)] * 2
               + [pltpu.VMEM((ATTN_STREAMS, tk, tq), BF16)] * 2)
    return pl.pallas_call(
        kernel, out_shape=jax.ShapeDtypeStruct((b, D_MODEL, s), BF16),
        grid=(b, n_groups, s // tq), in_specs=in_specs,
        out_specs=pl.BlockSpec((1, D_MODEL // n_groups, tq), lambda i, g, qi: (i, g, qi)),
        scratch_shapes=scratch, compiler_params=_params(3), name=name)(q_t, k, v_t, *extra)


def _post_call(o_t, gate_t, x_t, wout_t, post_gain):
    b, _, s = x_t.shape
    tok, _ = _tiles(s)
    tile = pl.BlockSpec((1, D_MODEL, tok), lambda i, t: (i, 0, t))
    consts = (wout_t, _col(post_gain))
    return pl.pallas_call(
        _post_kernel, out_shape=jax.ShapeDtypeStruct((b, D_MODEL, s), F32),
        grid=(b, s // tok), in_specs=[tile, tile, tile] + [_const_spec(c) for c in consts],
        out_specs=tile, compiler_params=_params(2), name="post")(o_t, gate_t, x_t, *consts)


def _rope_tables(pos, dim, theta):
    inv = 1.0 / (theta ** (jnp.arange(0, dim, 2, dtype=F32) / dim))
    ang = pos[:, None] * inv[None, :]
    return jnp.cos(ang).T, jnp.sin(ang).T


def _mla_layer(x_t, p):
    s = x_t.shape[2]
    w_ukv = p["w_ukv"].reshape(MLA_KV_LORA, MLA_HEADS, MLA_NOPE + MLA_V)
    wv_t = w_ukv[:, :, MLA_NOPE:].reshape(MLA_KV_LORA, MLA_HEADS * MLA_V).T
    wk = jnp.zeros((2 * LANES, MLA_HEADS, DK), F32)
    wk = wk.at[0:MLA_KV_LORA, :, 0:MLA_NOPE].set(w_ukv[:, :, 0:MLA_NOPE])
    eye = jnp.broadcast_to(jnp.eye(MLA_ROPE, dtype=F32)[:, None, :], (MLA_ROPE, MLA_HEADS, MLA_ROPE))
    wk = wk.at[LANES:LANES + MLA_ROPE, :, MLA_NOPE:MLA_QK].set(eye)
    consts = (_col(p["pre_norm"]), p["w_in"].T.astype(BF16), _col(p["q_norm"]),
              p["w_uq"].T.astype(BF16), _col(p["kv_norm"]), wv_t.astype(BF16),
              wk.reshape(2 * LANES, MLA_HEADS * DK).astype(BF16))
    tables = _rope_tables(jnp.arange(s, dtype=F32), MLA_ROPE, MLA_ROPE_THETA)
    q_t, k, v_t, gate_t = _pre_call(_mla_pre_kernel, "mla_pre", x_t, consts, tables,
                                    MLA_HEADS, MLA_HEADS, MLA_HEADS, MLA_V)
    o_t = _attn_call(_mla_attn_kernel, "mla_attn", q_t, k, v_t, (), ATTN_STREAMS,
                     lambda g: g, ATTN_STREAMS)
    return _post_call(o_t, gate_t, x_t, p["w_out"].T.astype(BF16), p["post_norm"])


def _gqa_layer(x_t, p):
    s = x_t.shape[2]
    consts = (_col(p["pre_norm"]), p["w_in"].T.astype(BF16), _col(p["q_norm"]), _col(p["k_norm"]))
    t_idx = jnp.arange(s)
    half = GQA_HD // 2
    tables = (_rope_tables((t_idx // GRID_W).astype(F32), half, GQA_ROPE_THETA)
              + _rope_tables((t_idx % GRID_W).astype(F32), half, GQA_ROPE_THETA))
    q_t, k, v_t, gate_t = _pre_call(_gqa_pre_kernel, "gqa_pre", x_t, consts, tables,
                                    GQA_HEADS, GQA_KV_HEADS * GQA_HD // DK, GQA_KV_HEADS, GQA_HD)
    assert ATTN_STREAMS == GQA_GROUP
    o_t = _attn_call(_gqa_attn_kernel, "gqa_attn", q_t, k, v_t, (), 1, lambda g: g // 2, 1)
    return _post_call(o_t, gate_t, x_t, p["w_out"].T.astype(BF16), p["post_norm"])


def _diff_layer(x_t, p, layer_idx):
    lambda_init = 0.8 - 0.6 * math.exp(-0.3 * layer_idx)
    width = DIFF_HEADS * 2 * DIFF_HD
    w_in = p["w_in"]
    w_qvg = jnp.concatenate([w_in[:, 0:width], w_in[:, 2 * width:4 * width]], axis=1)
    consts = (_col(p["pre_norm"]), w_qvg.T.astype(BF16), w_in[:, width:2 * width].astype(BF16))
    q_t, k, v_t, gate_t = _pre_call(_diff_pre_kernel, "diff_pre", x_t, consts, (),
                                    2 * DIFF_HEADS, DIFF_HEADS, DIFF_HEADS, 2 * DIFF_HD)
    row = lambda v: v.astype(F32).reshape(1, -1)
    extra = (row(p["lambda_q1"]), row(p["lambda_k1"]), row(p["lambda_q2"]), row(p["lambda_k2"]),
             _col(p["subln"]))
    o_t = _attn_call(functools.partial(_diff_attn_kernel, lambda_init=lambda_init), "diff_attn",
                     q_t, k, v_t, extra, ATTN_STREAMS // 2, lambda g: g, ATTN_STREAMS // 2)
    return _post_call(o_t, gate_t, x_t, p["w_out"].T.astype(BF16), p["post_norm"])


def _trunk(x, layers):
    x_t = jnp.swapaxes(x, 1, 2)
    for i, p in enumerate(layers):
        kind = i % 3
        if kind == 0:
            x_t = _mla_layer(x_t, p)
        elif kind == 1:
            x_t = _gqa_layer(x_t, p)
        else:
            x_t = _diff_layer(x_t, p, i)
    return jnp.swapaxes(x_t, 1, 2)


def kernel(x_prompt, x_sample, l0_pre_norm, l0_w_in, l0_q_norm, l0_w_uq, l0_kv_norm, l0_w_ukv, l0_w_out, l0_post_norm, l1_pre_norm, l1_w_in, l1_q_norm, l1_k_norm, l1_w_out, l1_post_norm, l2_pre_norm, l2_w_in, l2_lambda_q1, l2_lambda_k1, l2_lambda_q2, l2_lambda_k2, l2_subln, l2_w_out, l2_post_norm, l3_pre_norm, l3_w_in, l3_q_norm, l3_w_uq, l3_kv_norm, l3_w_ukv, l3_w_out, l3_post_norm):
    layers = [
        dict(pre_norm=l0_pre_norm, w_in=l0_w_in, q_norm=l0_q_norm, w_uq=l0_w_uq,
             kv_norm=l0_kv_norm, w_ukv=l0_w_ukv, w_out=l0_w_out, post_norm=l0_post_norm),
        dict(pre_norm=l1_pre_norm, w_in=l1_w_in, q_norm=l1_q_norm, k_norm=l1_k_norm,
             w_out=l1_w_out, post_norm=l1_post_norm),
        dict(pre_norm=l2_pre_norm, w_in=l2_w_in, lambda_q1=l2_lambda_q1, lambda_k1=l2_lambda_k1,
             lambda_q2=l2_lambda_q2, lambda_k2=l2_lambda_k2, subln=l2_subln,
             w_out=l2_w_out, post_norm=l2_post_norm),
        dict(pre_norm=l3_pre_norm, w_in=l3_w_in, q_norm=l3_q_norm, w_uq=l3_w_uq,
             kv_norm=l3_kv_norm, w_ukv=l3_w_ukv, w_out=l3_w_out, post_norm=l3_post_norm),
    ]
    return (_trunk(x_prompt, layers), _trunk(x_sample, layers))
```

```python
import functools
import math

import jax
import jax.numpy as jnp
from jax import lax
from jax.experimental import pallas as pl
from jax.experimental.pallas import tpu as pltpu

F32 = jnp.float32
BF16 = jnp.bfloat16

D_MODEL = 1024
NORM_EPS = 1e-6
GRID_W = 64

MLA_HEADS = 16
MLA_Q_LORA = 256
MLA_KV_LORA = 128
MLA_NOPE = 64
MLA_ROPE = 32
MLA_V = 64
MLA_QK = MLA_NOPE + MLA_ROPE
MLA_ROPE_THETA = 10000.0

GQA_HEADS = 16
GQA_KV_HEADS = 4
GQA_GROUP = GQA_HEADS // GQA_KV_HEADS
GQA_HD = 64
GQA_ROPE_THETA = 10000.0

DIFF_HEADS = 8
DIFF_HD = 64

LOG2_E = math.log2(math.e)

LANES = 128
BF16_ROWS = 16
DK = LANES
ONES_ROWS = BF16_ROWS
TOK_TILE = 512
Q_TILE = 512
ATTN_STREAMS = 4
SCORE_LIMIT = 60.0
SUBLANES = 8
VMEM_LIMIT_BYTES = 56 * 1024 * 1024


def _rms_rows(x, gain_col):
    ms = jnp.mean(x * x, axis=0, keepdims=True)
    return x * lax.rsqrt(ms + NORM_EPS) * gain_col


def _rope_rows(x1, x2, cos, sin):
    return x1 * cos - x2 * sin, x1 * sin + x2 * cos


def _dot(a, b):
    return jnp.dot(a, b, preferred_element_type=F32)


def _store_k_blocks(k_nat, k_out, knorm_out):
    worst = None
    for j in range(k_out.shape[1]):
        kb = k_nat[:, j * DK:(j + 1) * DK].astype(BF16)
        k_out[0, j] = kb
        kf = kb.astype(F32)
        n2 = jnp.sum(kf * kf, axis=1, keepdims=True)
        worst = n2 if worst is None else jnp.maximum(worst, n2)
    knorm_out[0, 0] = jnp.broadcast_to(jnp.max(worst, axis=0, keepdims=True),
                                       knorm_out.shape[2:])


def _mla_pre_kernel(x_ref, g_ref, win_ref, qn_ref, wuq_ref, kvn_ref, wv_ref, wk_ref,
                    cos_ref, sin_ref, q_out, k_out, v_out, gate_out, knorm_out):
    t = x_ref.shape[2]
    h = _rms_rows(x_ref[0], g_ref[...]).astype(BF16)
    cos = cos_ref[...]
    sin = sin_ref[...]
    scale = MLA_QK ** -0.5 * LOG2_E

    cq = _dot(win_ref[0:MLA_Q_LORA, :], h)
    cqn = _rms_rows(cq, qn_ref[...]).astype(BF16)
    q_all = _dot(wuq_ref[...], cqn) * scale
    zeros_q = jnp.zeros((DK - MLA_QK, t), F32)
    half = MLA_ROPE // 2
    for hd in range(MLA_HEADS):
        base = hd * MLA_QK
        nope = q_all[base:base + MLA_NOPE]
        r1, r2 = _rope_rows(q_all[base + MLA_NOPE:base + MLA_NOPE + half],
                            q_all[base + MLA_NOPE + half:base + MLA_QK], cos, sin)
        q_out[0, hd] = jnp.concatenate([nope, r1, r2, zeros_q], axis=0).astype(BF16)

    kv_lo = MLA_Q_LORA
    kv_hi = MLA_Q_LORA + MLA_KV_LORA + MLA_ROPE
    ckr = _dot(win_ref[kv_lo:kv_hi, :], h)
    ckvn = _rms_rows(ckr[0:MLA_KV_LORA], kvn_ref[...])
    v_all = _dot(wv_ref[...], ckvn.astype(BF16))
    ones = jnp.ones((ONES_ROWS, t), BF16)
    for hd in range(MLA_HEADS):
        v_out[0, hd, 0, 0:MLA_V, :] = v_all[hd * MLA_V:(hd + 1) * MLA_V].astype(BF16)
        v_out[0, hd, 0, MLA_V:MLA_V + ONES_ROWS, :] = ones

    kr = ckr[MLA_KV_LORA:MLA_KV_LORA + MLA_ROPE]
    r1, r2 = _rope_rows(kr[0:half], kr[half:MLA_ROPE], cos, sin)
    krr = jnp.concatenate([r1, r2, jnp.zeros((LANES - MLA_ROPE, t), F32)], axis=0)
    z = jnp.concatenate([ckvn.T, krr.T], axis=1).astype(BF16)
    k_all = _dot(z, wk_ref[...])
    _store_k_blocks(k_all, k_out, knorm_out)

    gate_out[0] = _dot(win_ref[kv_hi:kv_hi + D_MODEL, :], h).astype(BF16)


def _gqa_pre_kernel(x_ref, g_ref, win_ref, qn_ref, kn_ref, cr_ref, sr_ref, cc_ref, sc_ref,
                    q_out, k_out, v_out, gate_out, knorm_out):
    t = x_ref.shape[2]
    h = _rms_rows(x_ref[0], g_ref[...]).astype(BF16)
    cr, sr, cc, sc = cr_ref[...], sr_ref[...], cc_ref[...], sc_ref[...]
    scale = GQA_HD ** -0.5 * LOG2_E
    n_q = GQA_HEADS * GQA_HD
    n_kv = GQA_KV_HEADS * GQA_HD
    qk = _dot(win_ref[0:n_q + n_kv, :], h)
    zeros_q = jnp.zeros((DK - GQA_HD, t), F32)
    quarter = GQA_HD // 4

    def norm_rope(blk, gain_col):
        xn = _rms_rows(blk, gain_col)
        a, b = _rope_rows(xn[0:quarter], xn[quarter:2 * quarter], cr, sr)
        c, d = _rope_rows(xn[2 * quarter:3 * quarter], xn[3 * quarter:4 * quarter], cc, sc)
        return jnp.concatenate([a, b, c, d], axis=0)

    for hd in range(GQA_HEADS):
        rq = norm_rope(qk[hd * GQA_HD:(hd + 1) * GQA_HD], qn_ref[...]) * scale
        kv = hd // GQA_GROUP
        parts = [rq, zeros_q] if kv % 2 == 0 else [zeros_q, rq]
        q_out[0, hd] = jnp.concatenate(parts, axis=0).astype(BF16)

    k_rows = [norm_rope(qk[n_q + kv * GQA_HD:n_q + (kv + 1) * GQA_HD], kn_ref[...])
              for kv in range(GQA_KV_HEADS)]
    k_nat = jnp.concatenate(k_rows, axis=0).T
    _store_k_blocks(k_nat, k_out, knorm_out)

    v = _dot(win_ref[n_q + n_kv:n_q + 2 * n_kv, :], h)
    ones = jnp.ones((ONES_ROWS, t), BF16)
    for kv in range(GQA_KV_HEADS):
        v_out[0, kv, 0, 0:GQA_HD, :] = v[kv * GQA_HD:(kv + 1) * GQA_HD].astype(BF16)
        v_out[0, kv, 0, GQA_HD:GQA_HD + ONES_ROWS, :] = ones

    g_lo = n_q + 2 * n_kv
    gate_out[0] = _dot(win_ref[g_lo:g_lo + D_MODEL, :], h).astype(BF16)


def _diff_pre_kernel(x_ref, g_ref, win_ref, wk_ref, q_out, k_out, v_out, gate_out, knorm_out):
    t = x_ref.shape[2]
    h = _rms_rows(x_ref[0], g_ref[...]).astype(BF16)
    scale = DIFF_HD ** -0.5 * LOG2_E
    width = DIFF_HEADS * 2 * DIFF_HD
    q = _dot(win_ref[0:width, :], h) * scale
    zeros_q = jnp.zeros((DK - DIFF_HD, t), F32)
    for hc in range(2 * DIFF_HEADS):
        blk = q[hc * DIFF_HD:(hc + 1) * DIFF_HD]
        parts = [blk, zeros_q] if hc % 2 == 0 else [zeros_q, blk]
        q_out[0, hc] = jnp.concatenate(parts, axis=0).astype(BF16)

    k_nat = lax.dot_general(h, wk_ref[...], (((0,), (0,)), ((), ())),
                            preferred_element_type=F32)
    _store_k_blocks(k_nat, k_out, knorm_out)

    v = _dot(win_ref[width:2 * width, :], h)
    dv = 2 * DIFF_HD
    ones = jnp.ones((ONES_ROWS, t), BF16)
    for hd in range(DIFF_HEADS):
        v_out[0, hd, 0, 0:dv, :] = v[hd * dv:(hd + 1) * dv].astype(BF16)
        v_out[0, hd, 0, dv:dv + ONES_ROWS, :] = ones

    gate_out[0] = _dot(win_ref[2 * width:3 * width, :], h).astype(BF16)


def _attend(q_ts, k_chunk, v_chunk, nkc, dv, scratch, bias_fn=None):
    n = len(q_ts)
    tq = q_ts[0].shape[1]
    s_slots, p_slots = scratch[0:2], scratch[2:4]
    dvx = dv + ONES_ROWS
    assert nkc >= 2 and nkc % 2 == 0

    def qk(c, s_ref):
        maxima = []
        for g in range(n):
            s = _dot(k_chunk(g, c), q_ts[g])
            if bias_fn is not None:
                s = s + bias_fn(g, c)
            s_ref[g] = s
            maxima.append(jnp.max(s, axis=0, keepdims=True))
        return tuple(maxima)

    def softmax(s_ref, s_max, p_ref, m):
        m_out, alpha = [], []
        for g in range(n):
            m_new = jnp.maximum(m[g], s_max[g])
            alpha.append(jnp.exp2(m[g] - m_new))
            p_ref[g] = jnp.exp2(s_ref[g] - m_new).astype(BF16)
            m_out.append(m_new)
        return tuple(m_out), tuple(alpha)

    def pv(c, p_ref, alpha, acc):
        return tuple(alpha[g] * acc[g] + _dot(v_chunk(g, c), p_ref[g]) for g in range(n))

    m = tuple(jnp.full((1, tq), -jnp.inf, F32) for _ in range(n))
    acc = tuple(jnp.zeros((dvx, tq), F32) for _ in range(n))
    max0 = qk(0, s_slots[0])
    max1 = qk(1, s_slots[1])
    m, alpha = softmax(s_slots[0], max0, p_slots[0], m)

    def pair(j, carry):
        m, alpha, acc, max1 = carry
        t = 2 * j + 1
        max0 = qk(t + 1, s_slots[0])
        acc = pv(t - 1, p_slots[0], alpha, acc)
        m, alpha = softmax(s_slots[1], max1, p_slots[1], m)
        max1 = qk(t + 2, s_slots[1])
        acc = pv(t, p_slots[1], alpha, acc)
        m, alpha = softmax(s_slots[0], max0, p_slots[0], m)
        return m, alpha, acc, max1

    m, alpha, acc, max1 = lax.fori_loop(0, (nkc - 2) // 2, pair, (m, alpha, acc, max1))
    acc = pv(nkc - 2, p_slots[0], alpha, acc)
    m, alpha = softmax(s_slots[1], max1, p_slots[1], m)
    acc = pv(nkc - 1, p_slots[1], alpha, acc)
    return [a[0:dv] / a[dv:dv + 1] for a in acc]


def _attend_bounded(q_ts, k_chunk, v_chunk, nkc, dv, scratch, bias_fn=None):
    n = len(q_ts)
    tq = q_ts[0].shape[1]
    p_slots = scratch[2:4]
    dvx = dv + ONES_ROWS
    assert nkc >= 2 and nkc % 2 == 0

    def qk(c, p_ref):
        for g in range(n):
            s = _dot(k_chunk(g, c), q_ts[g])
            if bias_fn is not None:
                s = s + bias_fn(g, c)
            p_ref[g] = jnp.exp2(s).astype(BF16)

    def pv(c, p_ref, acc):
        return tuple(acc[g] + _dot(v_chunk(g, c), p_ref[g]) for g in range(n))

    acc = tuple(jnp.zeros((dvx, tq), F32) for _ in range(n))
    qk(0, p_slots[0])

    def pair(j, acc):
        t = 2 * j + 1
        qk(t, p_slots[1])
        acc = pv(t - 1, p_slots[0], acc)
        qk(t + 1, p_slots[0])
        return pv(t, p_slots[1], acc)

    acc = lax.fori_loop(0, (nkc - 2) // 2, pair, acc, unroll=True)
    qk(nkc - 1, p_slots[1])
    acc = pv(nkc - 2, p_slots[0], acc)
    acc = pv(nkc - 1, p_slots[1], acc)
    return [a[0:dv] / a[dv:dv + 1] for a in acc]


def _key_rows(c, tk):
    return pl.ds(pl.multiple_of(c * tk, tk), tk)


def _attend_dispatch(q_ts, k_chunk, v_chunk, kn_ref, nkc, dv, scratch, finish, bias_fn=None):
    k_max2 = jnp.max(kn_ref[0])
    q_max2 = None
    for q_t in q_ts:
        qf = q_t.astype(F32)
        n2 = jnp.sum(qf * qf, axis=0, keepdims=True)
        q_max2 = n2 if q_max2 is None else jnp.maximum(q_max2, n2)
    bounded = jnp.max(q_max2) * k_max2 <= SCORE_LIMIT * SCORE_LIMIT

    @pl.when(bounded)
    def _():
        finish(_attend_bounded(q_ts, k_chunk, v_chunk, nkc, dv, scratch, bias_fn))

    @pl.when(jnp.logical_not(bounded))
    def _():
        finish(_attend(q_ts, k_chunk, v_chunk, nkc, dv, scratch, bias_fn))


def _mla_attn_kernel(q_ref, k_ref, v_ref, kn_ref, o_ref, *scratch):
    nkc, tk = v_ref.shape[2], v_ref.shape[4]

    def finish(outs):
        for g, o in enumerate(outs):
            o_ref[0, g * MLA_V:(g + 1) * MLA_V, :] = o.astype(BF16)

    _attend_dispatch([q_ref[0, g] for g in range(ATTN_STREAMS)],
                     lambda g, c: k_ref[0, g, _key_rows(c, tk), :],
                     lambda g, c: v_ref[0, g, c], kn_ref, nkc, MLA_V, scratch, finish)


def _gqa_attn_kernel(q_ref, k_ref, v_ref, kn_ref, o_ref, *scratch):
    nkc, tk = v_ref.shape[2], v_ref.shape[4]

    def finish(outs):
        for g, o in enumerate(outs):
            o_ref[0, g * GQA_HD:(g + 1) * GQA_HD, :] = o.astype(BF16)

    _attend_dispatch([q_ref[0, g] for g in range(ATTN_STREAMS)],
                     lambda g, c: k_ref[0, 0, _key_rows(c, tk), :],
                     lambda g, c: v_ref[0, 0, c], kn_ref, nkc, GQA_HD, scratch, finish)


def _diff_attn_kernel(q_ref, k_ref, v_ref, kn_ref, lq1_ref, lk1_ref, lq2_ref, lk2_ref,
                      subln_ref, o_ref, *scratch, lambda_init):
    tq = q_ref.shape[3]
    nkc, tk = v_ref.shape[2], v_ref.shape[4]
    heads = ATTN_STREAMS // 2
    dv = 2 * DIFF_HD
    lam = (jnp.exp(jnp.sum(lq1_ref[...] * lk1_ref[...], axis=1, keepdims=True))
           - jnp.exp(jnp.sum(lq2_ref[...] * lk2_ref[...], axis=1, keepdims=True))
           + lambda_init)
    q_pos = (pl.program_id(2) * tq + lax.broadcasted_iota(jnp.int32, (1, tq), 1)).astype(F32)
    k_iota = lax.broadcasted_iota(jnp.int32, (tk, 1), 0)
    slopes = []
    for hd in range(heads):
        exponent = jnp.full((1, 1), 126 - (pl.program_id(1) * heads + hd), jnp.int32)
        slopes.append(lax.bitcast_convert_type(exponent << 23, F32) * LOG2_E)

    def bias_fn(g, c):
        k_pos = (c * tk + k_iota).astype(F32)
        return -slopes[g // 2] * jnp.abs(q_pos - k_pos)

    def finish(outs):
        for hd in range(heads):
            o = outs[2 * hd] - lam * outs[2 * hd + 1]
            o = _rms_rows(o, subln_ref[...]) * (1.0 - lambda_init)
            o_ref[0, hd * dv:(hd + 1) * dv, :] = o.astype(BF16)

    _attend_dispatch([q_ref[0, g] for g in range(ATTN_STREAMS)],
                     lambda g, c: k_ref[0, g // 2, _key_rows(c, tk), :],
                     lambda g, c: v_ref[0, g // 2, c], kn_ref, nkc, dv, scratch, finish,
                     bias_fn)


def _post_kernel(o_ref, gate_ref, x_ref, wout_ref, g_ref, y_ref):
    gate = gate_ref[0].astype(F32)
    og = (o_ref[0].astype(F32) * (gate / (1.0 + jnp.exp(-gate)))).astype(BF16)
    m = _dot(wout_ref[...], og)
    y_ref[0] = x_ref[0] + _rms_rows(m, g_ref[...])


def _params(n_axes):
    return pltpu.CompilerParams(dimension_semantics=("arbitrary",) * n_axes,
                                vmem_limit_bytes=VMEM_LIMIT_BYTES)


def _const_spec(arr):
    nd = arr.ndim
    return pl.BlockSpec(arr.shape, lambda *_: (0,) * nd)


def _col(v):
    return v.astype(F32).reshape(-1, 1)


def _tiles(s):
    tok = min(TOK_TILE, s)
    tq = min(Q_TILE, s)
    assert s % tok == 0 and s % tq == 0
    return tok, tq


def _pre_call(kernel, name, x_t, consts, tables, n_heads, n_kblocks, n_vheads, dv):
    b, _, s = x_t.shape
    tok, _ = _tiles(s)
    dvx = dv + ONES_ROWS
    in_specs = ([pl.BlockSpec((1, D_MODEL, tok), lambda i, t: (i, 0, t))]
                + [_const_spec(c) for c in consts]
                + [pl.BlockSpec((tb.shape[0], tok), lambda i, t: (0, t)) for tb in tables])
    out_shape = (jax.ShapeDtypeStruct((b, n_heads, DK, s), BF16),
                 jax.ShapeDtypeStruct((b, n_kblocks, s, DK), BF16),
                 jax.ShapeDtypeStruct((b, n_vheads, s // tok, dvx, tok), BF16),
                 jax.ShapeDtypeStruct((b, D_MODEL, s), BF16),
                 jax.ShapeDtypeStruct((b, s // tok, SUBLANES, LANES), F32))
    out_specs = (pl.BlockSpec((1, n_heads, DK, tok), lambda i, t: (i, 0, 0, t)),
                 pl.BlockSpec((1, n_kblocks, tok, DK), lambda i, t: (i, 0, t, 0)),
                 pl.BlockSpec((1, n_vheads, 1, dvx, tok), lambda i, t: (i, 0, t, 0, 0)),
                 pl.BlockSpec((1, D_MODEL, tok), lambda i, t: (i, 0, t)),
                 pl.BlockSpec((1, 1, SUBLANES, LANES), lambda i, t: (i, t, 0, 0)))
    return pl.pallas_call(kernel, out_shape=out_shape, grid=(b, s // tok), in_specs=in_specs,
                          out_specs=out_specs, compiler_params=_params(2), name=name)(
                              x_t, *consts, *tables)


def _attn_call(kernel, name, q_t, k, v_t, k_norm2, extra, k_blocks, k_block_of_group, v_heads):
    b, n_heads, _, s = q_t.shape
    _, tq = _tiles(s)
    nkc, dvx, tk = v_t.shape[2:]
    n_groups = n_heads // ATTN_STREAMS
    assert n_groups * v_heads == v_t.shape[1]
    in_specs = ([pl.BlockSpec((1, ATTN_STREAMS, DK, tq), lambda i, g, qi: (i, g, 0, qi)),
                 pl.BlockSpec((1, k_blocks, s, DK),
                              lambda i, g, qi: (i, k_block_of_group(g), 0, 0)),
                 pl.BlockSpec((1, v_heads, nkc, dvx, tk), lambda i, g, qi: (i, g, 0, 0, 0)),
                 pl.BlockSpec((1,) + k_norm2.shape[1:], lambda i, g, qi: (i, 0, 0, 0))]
                + [_const_spec(e) for e in extra])
    scratch = ([pltpu.VMEM((ATTN_STREAMS, tk, tq), F32)] * 2
               + [pltpu.VMEM((ATTN_STREAMS, tk, tq), BF16)] * 2)
    return pl.pallas_call(
        kernel, out_shape=jax.ShapeDtypeStruct((b, D_MODEL, s), BF16),
        grid=(b, n_groups, s // tq), in_specs=in_specs,
        out_specs=pl.BlockSpec((1, D_MODEL // n_groups, tq), lambda i, g, qi: (i, g, qi)),
        scratch_shapes=scratch, compiler_params=_params(3), name=name)(
            q_t, k, v_t, k_norm2, *extra)


def _post_call(o_t, gate_t, x_t, wout_t, post_gain):
    b, _, s = x_t.shape
    tok, _ = _tiles(s)
    tile = pl.BlockSpec((1, D_MODEL, tok), lambda i, t: (i, 0, t))
    consts = (wout_t, _col(post_gain))
    return pl.pallas_call(
        _post_kernel, out_shape=jax.ShapeDtypeStruct((b, D_MODEL, s), F32),
        grid=(b, s // tok), in_specs=[tile, tile, tile] + [_const_spec(c) for c in consts],
        out_specs=tile, compiler_params=_params(2), name="post")(o_t, gate_t, x_t, *consts)


def _rope_tables(pos, dim, theta):
    inv = 1.0 / (theta ** (jnp.arange(0, dim, 2, dtype=F32) / dim))
    ang = pos[:, None] * inv[None, :]
    return jnp.cos(ang).T, jnp.sin(ang).T


def _mla_layer(x_t, p):
    s = x_t.shape[2]
    w_ukv = p["w_ukv"].reshape(MLA_KV_LORA, MLA_HEADS, MLA_NOPE + MLA_V)
    wv_t = w_ukv[:, :, MLA_NOPE:].reshape(MLA_KV_LORA, MLA_HEADS * MLA_V).T
    wk = jnp.zeros((2 * LANES, MLA_HEADS, DK), F32)
    wk = wk.at[0:MLA_KV_LORA, :, 0:MLA_NOPE].set(w_ukv[:, :, 0:MLA_NOPE])
    eye = jnp.broadcast_to(jnp.eye(MLA_ROPE, dtype=F32)[:, None, :], (MLA_ROPE, MLA_HEADS, MLA_ROPE))
    wk = wk.at[LANES:LANES + MLA_ROPE, :, MLA_NOPE:MLA_QK].set(eye)
    consts = (_col(p["pre_norm"]), p["w_in"].T.astype(BF16), _col(p["q_norm"]),
              p["w_uq"].T.astype(BF16), _col(p["kv_norm"]), wv_t.astype(BF16),
              wk.reshape(2 * LANES, MLA_HEADS * DK).astype(BF16))
    tables = _rope_tables(jnp.arange(s, dtype=F32), MLA_ROPE, MLA_ROPE_THETA)
    q_t, k, v_t, gate_t, kn2 = _pre_call(_mla_pre_kernel, "mla_pre", x_t, consts, tables,
                                         MLA_HEADS, MLA_HEADS, MLA_HEADS, MLA_V)
    o_t = _attn_call(_mla_attn_kernel, "mla_attn", q_t, k, v_t, kn2, (), ATTN_STREAMS,
                     lambda g: g, ATTN_STREAMS)
    return _post_call(o_t, gate_t, x_t, p["w_out"].T.astype(BF16), p["post_norm"])


def _gqa_layer(x_t, p):
    s = x_t.shape[2]
    consts = (_col(p["pre_norm"]), p["w_in"].T.astype(BF16), _col(p["q_norm"]), _col(p["k_norm"]))
    t_idx = jnp.arange(s)
    half = GQA_HD // 2
    tables = (_rope_tables((t_idx // GRID_W).astype(F32), half, GQA_ROPE_THETA)
              + _rope_tables((t_idx % GRID_W).astype(F32), half, GQA_ROPE_THETA))
    q_t, k, v_t, gate_t, kn2 = _pre_call(_gqa_pre_kernel, "gqa_pre", x_t, consts, tables, GQA_HEADS,
                                         GQA_KV_HEADS * GQA_HD // DK, GQA_KV_HEADS, GQA_HD)
    assert ATTN_STREAMS == GQA_GROUP
    o_t = _attn_call(_gqa_attn_kernel, "gqa_attn", q_t, k, v_t, kn2, (), 1, lambda g: g // 2, 1)
    return _post_call(o_t, gate_t, x_t, p["w_out"].T.astype(BF16), p["post_norm"])


def _diff_layer(x_t, p, layer_idx):
    lambda_init = 0.8 - 0.6 * math.exp(-0.3 * layer_idx)
    width = DIFF_HEADS * 2 * DIFF_HD
    w_in = p["w_in"]
    w_qvg = jnp.concatenate([w_in[:, 0:width], w_in[:, 2 * width:4 * width]], axis=1)
    consts = (_col(p["pre_norm"]), w_qvg.T.astype(BF16), w_in[:, width:2 * width].astype(BF16))
    q_t, k, v_t, gate_t, kn2 = _pre_call(_diff_pre_kernel, "diff_pre", x_t, consts, (),
                                         2 * DIFF_HEADS, DIFF_HEADS, DIFF_HEADS, 2 * DIFF_HD)
    row = lambda v: v.astype(F32).reshape(1, -1)
    extra = (row(p["lambda_q1"]), row(p["lambda_k1"]), row(p["lambda_q2"]), row(p["lambda_k2"]),
             _col(p["subln"]))
    o_t = _attn_call(functools.partial(_diff_attn_kernel, lambda_init=lambda_init), "diff_attn",
                     q_t, k, v_t, kn2, extra, ATTN_STREAMS // 2, lambda g: g, ATTN_STREAMS // 2)
    return _post_call(o_t, gate_t, x_t, p["w_out"].T.astype(BF16), p["post_norm"])


def _trunk(x, layers):
    x_t = jnp.swapaxes(x, 1, 2)
    for i, p in enumerate(layers):
        kind = i % 3
        if kind == 0:
            x_t = _mla_layer(x_t, p)
        elif kind == 1:
            x_t = _gqa_layer(x_t, p)
        else:
            x_t = _diff_layer(x_t, p, i)
    return jnp.swapaxes(x_t, 1, 2)


def kernel(x_prompt, x_sample, l0_pre_norm, l0_w_in, l0_q_norm, l0_w_uq, l0_kv_norm, l0_w_ukv, l0_w_out, l0_post_norm, l1_pre_norm, l1_w_in, l1_q_norm, l1_k_norm, l1_w_out, l1_post_norm, l2_pre_norm, l2_w_in, l2_lambda_q1, l2_lambda_k1, l2_lambda_q2, l2_lambda_k2, l2_subln, l2_w_out, l2_post_norm, l3_pre_norm, l3_w_in, l3_q_norm, l3_w_uq, l3_kv_norm, l3_w_ukv, l3_w_out, l3_post_norm):
    layers = [
        dict(pre_norm=l0_pre_norm, w_in=l0_w_in, q_norm=l0_q_norm, w_uq=l0_w_uq,
             kv_norm=l0_kv_norm, w_ukv=l0_w_ukv, w_out=l0_w_out, post_norm=l0_post_norm),
        dict(pre_norm=l1_pre_norm, w_in=l1_w_in, q_norm=l1_q_norm, k_norm=l1_k_norm,
             w_out=l1_w_out, post_norm=l1_post_norm),
        dict(pre_norm=l2_pre_norm, w_in=l2_w_in, lambda_q1=l2_lambda_q1, lambda_k1=l2_lambda_k1,
             lambda_q2=l2_lambda_q2, lambda_k2=l2_lambda_k2, subln=l2_subln,
             w_out=l2_w_out, post_norm=l2_post_norm),
        dict(pre_norm=l3_pre_norm, w_in=l3_w_in, q_norm=l3_q_norm, w_uq=l3_w_uq,
             kv_norm=l3_kv_norm, w_ukv=l3_w_ukv, w_out=l3_w_out, post_norm=l3_post_norm),
    ]
    return (_trunk(x_prompt, layers), _trunk(x_sample, layers))
```

```python
import functools
import math

import jax
import jax.numpy as jnp
from jax import lax
from jax.experimental import pallas as pl
from jax.experimental.pallas import tpu as pltpu

F32 = jnp.float32
BF16 = jnp.bfloat16

D_MODEL = 1024
NORM_EPS = 1e-6
GRID_W = 64

MLA_HEADS = 16
MLA_Q_LORA = 256
MLA_KV_LORA = 128
MLA_NOPE = 64
MLA_ROPE = 32
MLA_V = 64
MLA_QK = MLA_NOPE + MLA_ROPE
MLA_ROPE_THETA = 10000.0

GQA_HEADS = 16
GQA_KV_HEADS = 4
GQA_GROUP = GQA_HEADS // GQA_KV_HEADS
GQA_HD = 64
GQA_ROPE_THETA = 10000.0

DIFF_HEADS = 8
DIFF_HD = 64

LOG2_E = math.log2(math.e)

LANES = 128
BF16_ROWS = 16
DK = LANES
ONES_ROWS = BF16_ROWS
TOK_TILE = 512
Q_TILE = 512
ATTN_STREAMS = 4
SCORE_LIMIT = 60.0
SUBLANES = 8
VMEM_LIMIT_BYTES = 56 * 1024 * 1024


def _rms_rows(x, gain_col):
    ms = jnp.mean(x * x, axis=0, keepdims=True)
    return x * lax.rsqrt(ms + NORM_EPS) * gain_col


def _rope_rows(x1, x2, cos, sin):
    return x1 * cos - x2 * sin, x1 * sin + x2 * cos


def _dot(a, b):
    return jnp.dot(a, b, preferred_element_type=F32)


def _feature_major(x_ref, token_major):
    return x_ref[0].T if token_major else x_ref[0]


def _store_norm_tile(worst, norm_out):
    top = jnp.max(jnp.max(worst, axis=0, keepdims=True), axis=1, keepdims=True)
    norm_out[0, 0] = jnp.broadcast_to(top, norm_out.shape[2:])


def _store_q_block(q_f32, q_out, head, worst):
    qb = q_f32.astype(BF16)
    q_out[0, head] = qb
    qf = qb.astype(F32)
    n2 = jnp.sum(qf * qf, axis=0, keepdims=True)
    return n2 if worst is None else jnp.maximum(worst, n2)


def _store_k_blocks(k_nat, k_out, knorm_out):
    worst = None
    for j in range(k_out.shape[1]):
        kb = k_nat[:, j * DK:(j + 1) * DK].astype(BF16)
        k_out[0, j] = kb
        kf = kb.astype(F32)
        n2 = jnp.sum(kf * kf, axis=1, keepdims=True)
        worst = n2 if worst is None else jnp.maximum(worst, n2)
    _store_norm_tile(worst, knorm_out)


def _mla_pre_kernel(x_ref, g_ref, win_ref, qn_ref, wuq_ref, kvn_ref, wv_ref, wk_ref,
                    cos_ref, sin_ref, q_out, k_out, v_out, gate_out, qnorm_out, knorm_out,
                    *, x_token_major):
    x = _feature_major(x_ref, x_token_major)
    t = x.shape[1]
    h = _rms_rows(x, g_ref[...]).astype(BF16)
    cos = cos_ref[...]
    sin = sin_ref[...]
    scale = MLA_QK ** -0.5 * LOG2_E

    cq = _dot(win_ref[0:MLA_Q_LORA, :], h)
    cqn = _rms_rows(cq, qn_ref[...]).astype(BF16)
    q_all = _dot(wuq_ref[...], cqn) * scale
    zeros_q = jnp.zeros((DK - MLA_QK, t), F32)
    half = MLA_ROPE // 2
    worst = None
    for hd in range(MLA_HEADS):
        base = hd * MLA_QK
        nope = q_all[base:base + MLA_NOPE]
        r1, r2 = _rope_rows(q_all[base + MLA_NOPE:base + MLA_NOPE + half],
                            q_all[base + MLA_NOPE + half:base + MLA_QK], cos, sin)
        worst = _store_q_block(jnp.concatenate([nope, r1, r2, zeros_q], axis=0), q_out, hd, worst)
    _store_norm_tile(worst, qnorm_out)

    kv_lo = MLA_Q_LORA
    kv_hi = MLA_Q_LORA + MLA_KV_LORA + MLA_ROPE
    ckr = _dot(win_ref[kv_lo:kv_hi, :], h)
    ckvn = _rms_rows(ckr[0:MLA_KV_LORA], kvn_ref[...])
    v_all = _dot(wv_ref[...], ckvn.astype(BF16))
    ones = jnp.ones((ONES_ROWS, t), BF16)
    for hd in range(MLA_HEADS):
        v_out[0, hd, 0, 0:MLA_V, :] = v_all[hd * MLA_V:(hd + 1) * MLA_V].astype(BF16)
        v_out[0, hd, 0, MLA_V:MLA_V + ONES_ROWS, :] = ones

    kr = ckr[MLA_KV_LORA:MLA_KV_LORA + MLA_ROPE]
    r1, r2 = _rope_rows(kr[0:half], kr[half:MLA_ROPE], cos, sin)
    krr = jnp.concatenate([r1, r2, jnp.zeros((LANES - MLA_ROPE, t), F32)], axis=0)
    z = jnp.concatenate([ckvn.T, krr.T], axis=1).astype(BF16)
    k_all = _dot(z, wk_ref[...])
    _store_k_blocks(k_all, k_out, knorm_out)

    gate_out[0] = _dot(win_ref[kv_hi:kv_hi + D_MODEL, :], h).astype(BF16)


def _gqa_pre_kernel(x_ref, g_ref, win_ref, qn_ref, kn_ref, cr_ref, sr_ref, cc_ref, sc_ref,
                    q_out, k_out, v_out, gate_out, qnorm_out, knorm_out, *, x_token_major):
    x = _feature_major(x_ref, x_token_major)
    t = x.shape[1]
    h = _rms_rows(x, g_ref[...]).astype(BF16)
    cr, sr, cc, sc = cr_ref[...], sr_ref[...], cc_ref[...], sc_ref[...]
    scale = GQA_HD ** -0.5 * LOG2_E
    n_q = GQA_HEADS * GQA_HD
    n_kv = GQA_KV_HEADS * GQA_HD
    qk = _dot(win_ref[0:n_q + n_kv, :], h)
    zeros_q = jnp.zeros((DK - GQA_HD, t), F32)
    quarter = GQA_HD // 4

    def norm_rope(blk, gain_col):
        xn = _rms_rows(blk, gain_col)
        a, b = _rope_rows(xn[0:quarter], xn[quarter:2 * quarter], cr, sr)
        c, d = _rope_rows(xn[2 * quarter:3 * quarter], xn[3 * quarter:4 * quarter], cc, sc)
        return jnp.concatenate([a, b, c, d], axis=0)

    worst = None
    for hd in range(GQA_HEADS):
        rq = norm_rope(qk[hd * GQA_HD:(hd + 1) * GQA_HD], qn_ref[...]) * scale
        kv = hd // GQA_GROUP
        parts = [rq, zeros_q] if kv % 2 == 0 else [zeros_q, rq]
        worst = _store_q_block(jnp.concatenate(parts, axis=0), q_out, hd, worst)
    _store_norm_tile(worst, qnorm_out)

    k_rows = [norm_rope(qk[n_q + kv * GQA_HD:n_q + (kv + 1) * GQA_HD], kn_ref[...])
              for kv in range(GQA_KV_HEADS)]
    k_nat = jnp.concatenate(k_rows, axis=0).T
    _store_k_blocks(k_nat, k_out, knorm_out)

    v = _dot(win_ref[n_q + n_kv:n_q + 2 * n_kv, :], h)
    ones = jnp.ones((ONES_ROWS, t), BF16)
    for kv in range(GQA_KV_HEADS):
        v_out[0, kv, 0, 0:GQA_HD, :] = v[kv * GQA_HD:(kv + 1) * GQA_HD].astype(BF16)
        v_out[0, kv, 0, GQA_HD:GQA_HD + ONES_ROWS, :] = ones

    g_lo = n_q + 2 * n_kv
    gate_out[0] = _dot(win_ref[g_lo:g_lo + D_MODEL, :], h).astype(BF16)


def _diff_pre_kernel(x_ref, g_ref, win_ref, wk_ref, q_out, k_out, v_out, gate_out, qnorm_out,
                     knorm_out, *, x_token_major):
    x = _feature_major(x_ref, x_token_major)
    t = x.shape[1]
    h = _rms_rows(x, g_ref[...]).astype(BF16)
    scale = DIFF_HD ** -0.5 * LOG2_E
    width = DIFF_HEADS * 2 * DIFF_HD
    q = _dot(win_ref[0:width, :], h) * scale
    zeros_q = jnp.zeros((DK - DIFF_HD, t), F32)
    worst = None
    for hc in range(2 * DIFF_HEADS):
        blk = q[hc * DIFF_HD:(hc + 1) * DIFF_HD]
        parts = [blk, zeros_q] if hc % 2 == 0 else [zeros_q, blk]
        worst = _store_q_block(jnp.concatenate(parts, axis=0), q_out, hc, worst)
    _store_norm_tile(worst, qnorm_out)

    k_nat = lax.dot_general(h, wk_ref[...], (((0,), (0,)), ((), ())),
                            preferred_element_type=F32)
    _store_k_blocks(k_nat, k_out, knorm_out)

    v = _dot(win_ref[width:2 * width, :], h)
    dv = 2 * DIFF_HD
    ones = jnp.ones((ONES_ROWS, t), BF16)
    for hd in range(DIFF_HEADS):
        v_out[0, hd, 0, 0:dv, :] = v[hd * dv:(hd + 1) * dv].astype(BF16)
        v_out[0, hd, 0, dv:dv + ONES_ROWS, :] = ones

    gate_out[0] = _dot(win_ref[2 * width:3 * width, :], h).astype(BF16)


def _attend(q_ts, k_chunk, v_chunk, nkc, dv, scratch, bias_fn=None):
    n = len(q_ts)
    tq = q_ts[0].shape[1]
    s_slots, p_slots = scratch[0:2], scratch[2:4]
    dvx = dv + ONES_ROWS
    assert nkc >= 2 and nkc % 2 == 0

    def qk(c, s_ref):
        maxima = []
        for g in range(n):
            s = _dot(k_chunk(g, c), q_ts[g])
            if bias_fn is not None:
                s = s + bias_fn(g, c)
            s_ref[g] = s
            maxima.append(jnp.max(s, axis=0, keepdims=True))
        return tuple(maxima)

    def softmax(s_ref, s_max, p_ref, m):
        m_out, alpha = [], []
        for g in range(n):
            m_new = jnp.maximum(m[g], s_max[g])
            alpha.append(jnp.exp2(m[g] - m_new))
            p_ref[g] = jnp.exp2(s_ref[g] - m_new).astype(BF16)
            m_out.append(m_new)
        return tuple(m_out), tuple(alpha)

    def pv(c, p_ref, alpha, acc):
        return tuple(alpha[g] * acc[g] + _dot(v_chunk(g, c), p_ref[g]) for g in range(n))

    m = tuple(jnp.full((1, tq), -jnp.inf, F32) for _ in range(n))
    acc = tuple(jnp.zeros((dvx, tq), F32) for _ in range(n))
    max0 = qk(0, s_slots[0])
    max1 = qk(1, s_slots[1])
    m, alpha = softmax(s_slots[0], max0, p_slots[0], m)

    def pair(j, carry):
        m, alpha, acc, max1 = carry
        t = 2 * j + 1
        max0 = qk(t + 1, s_slots[0])
        acc = pv(t - 1, p_slots[0], alpha, acc)
        m, alpha = softmax(s_slots[1], max1, p_slots[1], m)
        max1 = qk(t + 2, s_slots[1])
        acc = pv(t, p_slots[1], alpha, acc)
        m, alpha = softmax(s_slots[0], max0, p_slots[0], m)
        return m, alpha, acc, max1

    m, alpha, acc, max1 = lax.fori_loop(0, (nkc - 2) // 2, pair, (m, alpha, acc, max1))
    acc = pv(nkc - 2, p_slots[0], alpha, acc)
    m, alpha = softmax(s_slots[1], max1, p_slots[1], m)
    acc = pv(nkc - 1, p_slots[1], alpha, acc)
    return [a[0:dv] * (1.0 / a[dv:dv + 1]) for a in acc]


def _attend_bounded(q_ts, k_chunk, v_chunk, nkc, dv, scratch, bias_fn=None):
    n = len(q_ts)
    tq = q_ts[0].shape[1]
    p_slots = scratch[2:4]
    dvx = dv + ONES_ROWS
    assert nkc >= 2 and nkc % 2 == 0

    def qk(c, p_ref):
        for g in range(n):
            s = _dot(k_chunk(g, c), q_ts[g])
            if bias_fn is not None:
                s = s + bias_fn(g, c)
            p_ref[g] = jnp.exp2(s).astype(BF16)

    def pv(c, p_ref, acc):
        return tuple(acc[g] + _dot(v_chunk(g, c), p_ref[g]) for g in range(n))

    acc = tuple(jnp.zeros((dvx, tq), F32) for _ in range(n))
    qk(0, p_slots[0])

    def pair(j, acc):
        t = 2 * j + 1
        qk(t, p_slots[1])
        acc = pv(t - 1, p_slots[0], acc)
        qk(t + 1, p_slots[0])
        return pv(t, p_slots[1], acc)

    acc = lax.fori_loop(0, (nkc - 2) // 2, pair, acc, unroll=True)
    qk(nkc - 1, p_slots[1])
    acc = pv(nkc - 2, p_slots[0], acc)
    acc = pv(nkc - 1, p_slots[1], acc)
    return [a[0:dv] * (1.0 / a[dv:dv + 1]) for a in acc]


def _key_rows(c, tk):
    return pl.ds(pl.multiple_of(c * tk, tk), tk)


def _attend_dispatch(q_ts, k_chunk, v_chunk, norm_refs, nkc, dv, scratch, finish, bias_fn=None):
    qn_ref, kn_ref = norm_refs
    bounded = jnp.max(qn_ref[0]) * jnp.max(kn_ref[0]) <= SCORE_LIMIT * SCORE_LIMIT

    @pl.when(bounded)
    def _():
        finish(_attend_bounded(q_ts, k_chunk, v_chunk, nkc, dv, scratch, bias_fn))

    @pl.when(jnp.logical_not(bounded))
    def _():
        finish(_attend(q_ts, k_chunk, v_chunk, nkc, dv, scratch, bias_fn))


def _mla_attn_kernel(q_ref, k_ref, v_ref, qn2_ref, kn2_ref, o_ref, *scratch):
    nkc, tk = v_ref.shape[2], v_ref.shape[4]

    def finish(outs):
        for g, o in enumerate(outs):
            o_ref[0, g * MLA_V:(g + 1) * MLA_V, :] = o.astype(BF16)

    _attend_dispatch([q_ref[0, g] for g in range(ATTN_STREAMS)],
                     lambda g, c: k_ref[0, g, _key_rows(c, tk), :],
                     lambda g, c: v_ref[0, g, c], (qn2_ref, kn2_ref), nkc, MLA_V, scratch,
                     finish)


def _gqa_attn_kernel(q_ref, k_ref, v_ref, qn2_ref, kn2_ref, o_ref, *scratch):
    nkc, tk = v_ref.shape[2], v_ref.shape[4]

    def finish(outs):
        for g, o in enumerate(outs):
            o_ref[0, g * GQA_HD:(g + 1) * GQA_HD, :] = o.astype(BF16)

    _attend_dispatch([q_ref[0, g] for g in range(ATTN_STREAMS)],
                     lambda g, c: k_ref[0, 0, _key_rows(c, tk), :],
                     lambda g, c: v_ref[0, 0, c], (qn2_ref, kn2_ref), nkc, GQA_HD, scratch,
                     finish)


def _diff_attn_kernel(q_ref, k_ref, v_ref, qn2_ref, kn2_ref, lq1_ref, lk1_ref, lq2_ref, lk2_ref,
                      subln_ref, o_ref, *scratch, lambda_init):
    tq = q_ref.shape[3]
    nkc, tk = v_ref.shape[2], v_ref.shape[4]
    heads = ATTN_STREAMS // 2
    dv = 2 * DIFF_HD
    lam = (jnp.exp(jnp.sum(lq1_ref[...] * lk1_ref[...], axis=1, keepdims=True))
           - jnp.exp(jnp.sum(lq2_ref[...] * lk2_ref[...], axis=1, keepdims=True))
           + lambda_init)
    q_pos = (pl.program_id(2) * tq + lax.broadcasted_iota(jnp.int32, (1, tq), 1)).astype(F32)
    k_iota = lax.broadcasted_iota(jnp.int32, (tk, 1), 0)
    slopes = []
    for hd in range(heads):
        exponent = jnp.full((1, 1), 126 - (pl.program_id(1) * heads + hd), jnp.int32)
        slopes.append(lax.bitcast_convert_type(exponent << 23, F32) * LOG2_E)

    def bias_fn(g, c):
        k_pos = (c * tk + k_iota).astype(F32)
        return -slopes[g // 2] * jnp.abs(q_pos - k_pos)

    def finish(outs):
        for hd in range(heads):
            o = outs[2 * hd] - lam * outs[2 * hd + 1]
            o = _rms_rows(o, subln_ref[...]) * (1.0 - lambda_init)
            o_ref[0, hd * dv:(hd + 1) * dv, :] = o.astype(BF16)

    _attend_dispatch([q_ref[0, g] for g in range(ATTN_STREAMS)],
                     lambda g, c: k_ref[0, g // 2, _key_rows(c, tk), :],
                     lambda g, c: v_ref[0, g // 2, c], (qn2_ref, kn2_ref), nkc, dv, scratch, finish,
                     bias_fn)


def _post_kernel(o_ref, gate_ref, x_ref, wout_ref, g_ref, y_ref, *, x_token_major,
                 y_token_major):
    gate = gate_ref[0].astype(F32)
    og = (o_ref[0].astype(F32) * (gate / (1.0 + jnp.exp(-gate)))).astype(BF16)
    m = _dot(wout_ref[...], og)
    y = _feature_major(x_ref, x_token_major) + _rms_rows(m, g_ref[...])
    y_ref[0] = y.T if y_token_major else y


def _params(n_axes):
    return pltpu.CompilerParams(dimension_semantics=("arbitrary",) * n_axes,
                                vmem_limit_bytes=VMEM_LIMIT_BYTES)


def _const_spec(arr):
    nd = arr.ndim
    return pl.BlockSpec(arr.shape, lambda *_: (0,) * nd)


def _col(v):
    return v.astype(F32).reshape(-1, 1)


def _tiles(s):
    tok = min(TOK_TILE, s)
    tq = min(Q_TILE, s)
    assert s % tok == 0 and s % tq == 0
    return tok, tq


def _stream_dims(x, token_major):
    return (x.shape[0], x.shape[1]) if token_major else (x.shape[0], x.shape[2])


def _stream_spec(tok, token_major):
    if token_major:
        return pl.BlockSpec((1, tok, D_MODEL), lambda i, t: (i, t, 0))
    return pl.BlockSpec((1, D_MODEL, tok), lambda i, t: (i, 0, t))


def _pre_call(kernel, name, x, x_token_major, consts, tables, n_heads, n_kblocks, n_vheads, dv):
    b, s = _stream_dims(x, x_token_major)
    tok, _ = _tiles(s)
    dvx = dv + ONES_ROWS
    in_specs = ([_stream_spec(tok, x_token_major)]
                + [_const_spec(c) for c in consts]
                + [pl.BlockSpec((tb.shape[0], tok), lambda i, t: (0, t)) for tb in tables])
    norm_shape = jax.ShapeDtypeStruct((b, s // tok, SUBLANES, LANES), F32)
    norm_spec = pl.BlockSpec((1, 1, SUBLANES, LANES), lambda i, t: (i, t, 0, 0))
    out_shape = (jax.ShapeDtypeStruct((b, n_heads, DK, s), BF16),
                 jax.ShapeDtypeStruct((b, n_kblocks, s, DK), BF16),
                 jax.ShapeDtypeStruct((b, n_vheads, s // tok, dvx, tok), BF16),
                 jax.ShapeDtypeStruct((b, D_MODEL, s), BF16),
                 norm_shape, norm_shape)
    out_specs = (pl.BlockSpec((1, n_heads, DK, tok), lambda i, t: (i, 0, 0, t)),
                 pl.BlockSpec((1, n_kblocks, tok, DK), lambda i, t: (i, 0, t, 0)),
                 pl.BlockSpec((1, n_vheads, 1, dvx, tok), lambda i, t: (i, 0, t, 0, 0)),
                 pl.BlockSpec((1, D_MODEL, tok), lambda i, t: (i, 0, t)),
                 norm_spec, norm_spec)
    return pl.pallas_call(functools.partial(kernel, x_token_major=x_token_major),
                          out_shape=out_shape, grid=(b, s // tok), in_specs=in_specs,
                          out_specs=out_specs, compiler_params=_params(2), name=name)(
                              x, *consts, *tables)


def _attn_call(kernel, name, pre_outs, extra, k_blocks, k_block_of_group, v_heads):
    q_t, k, v_t, _, q_norm2, k_norm2 = pre_outs
    b, n_heads, _, s = q_t.shape
    tok, tq = _tiles(s)
    assert tq == tok
    nkc, dvx, tk = v_t.shape[2:]
    n_groups = n_heads // ATTN_STREAMS
    assert n_groups * v_heads == v_t.shape[1]
    in_specs = ([pl.BlockSpec((1, ATTN_STREAMS, DK, tq), lambda i, g, qi: (i, g, 0, qi)),
                 pl.BlockSpec((1, k_blocks, s, DK),
                              lambda i, g, qi: (i, k_block_of_group(g), 0, 0)),
                 pl.BlockSpec((1, v_heads, nkc, dvx, tk), lambda i, g, qi: (i, g, 0, 0, 0)),
                 pl.BlockSpec((1, 1, SUBLANES, LANES), lambda i, g, qi: (i, qi, 0, 0)),
                 pl.BlockSpec((1,) + k_norm2.shape[1:], lambda i, g, qi: (i, 0, 0, 0))]
                + [_const_spec(e) for e in extra])
    scratch = ([pltpu.VMEM((ATTN_STREAMS, tk, tq), F32)] * 2
               + [pltpu.VMEM((ATTN_STREAMS, tk, tq), BF16)] * 2)
    return pl.pallas_call(
        kernel, out_shape=jax.ShapeDtypeStruct((b, D_MODEL, s), BF16),
        grid=(b, n_groups, s // tq), in_specs=in_specs,
        out_specs=pl.BlockSpec((1, D_MODEL // n_groups, tq), lambda i, g, qi: (i, g, qi)),
        scratch_shapes=scratch, compiler_params=_params(3), name=name)(
            q_t, k, v_t, q_norm2, k_norm2, *extra)


def _post_call(o_t, gate_t, x, x_token_major, y_token_major, wout_t, post_gain):
    b, s = _stream_dims(x, x_token_major)
    tok, _ = _tiles(s)
    tile = _stream_spec(tok, False)
    consts = (wout_t, _col(post_gain))
    y_shape = (b, s, D_MODEL) if y_token_major else (b, D_MODEL, s)
    return pl.pallas_call(
        functools.partial(_post_kernel, x_token_major=x_token_major,
                          y_token_major=y_token_major),
        out_shape=jax.ShapeDtypeStruct(y_shape, F32), grid=(b, s // tok),
        in_specs=([tile, tile, _stream_spec(tok, x_token_major)]
                  + [_const_spec(c) for c in consts]),
        out_specs=_stream_spec(tok, y_token_major), compiler_params=_params(2), name="post")(
            o_t, gate_t, x, *consts)


def _rope_tables(pos, dim, theta):
    inv = 1.0 / (theta ** (jnp.arange(0, dim, 2, dtype=F32) / dim))
    ang = pos[:, None] * inv[None, :]
    return jnp.cos(ang).T, jnp.sin(ang).T


def _mla_layer(x, p, x_token_major=False, y_token_major=False):
    _, s = _stream_dims(x, x_token_major)
    w_ukv = p["w_ukv"].reshape(MLA_KV_LORA, MLA_HEADS, MLA_NOPE + MLA_V)
    wv_t = w_ukv[:, :, MLA_NOPE:].reshape(MLA_KV_LORA, MLA_HEADS * MLA_V).T
    wk = jnp.zeros((2 * LANES, MLA_HEADS, DK), F32)
    wk = wk.at[0:MLA_KV_LORA, :, 0:MLA_NOPE].set(w_ukv[:, :, 0:MLA_NOPE])
    eye = jnp.broadcast_to(jnp.eye(MLA_ROPE, dtype=F32)[:, None, :], (MLA_ROPE, MLA_HEADS, MLA_ROPE))
    wk = wk.at[LANES:LANES + MLA_ROPE, :, MLA_NOPE:MLA_QK].set(eye)
    consts = (_col(p["pre_norm"]), p["w_in"].T.astype(BF16), _col(p["q_norm"]),
              p["w_uq"].T.astype(BF16), _col(p["kv_norm"]), wv_t.astype(BF16),
              wk.reshape(2 * LANES, MLA_HEADS * DK).astype(BF16))
    tables = _rope_tables(jnp.arange(s, dtype=F32), MLA_ROPE, MLA_ROPE_THETA)
    pre = _pre_call(_mla_pre_kernel, "mla_pre", x, x_token_major, consts, tables,
                    MLA_HEADS, MLA_HEADS, MLA_HEADS, MLA_V)
    o_t = _attn_call(_mla_attn_kernel, "mla_attn", pre, (), ATTN_STREAMS, lambda g: g,
                     ATTN_STREAMS)
    return _post_call(o_t, pre[3], x, x_token_major, y_token_major, p["w_out"].T.astype(BF16),
                      p["post_norm"])


def _gqa_layer(x, p, x_token_major=False, y_token_major=False):
    _, s = _stream_dims(x, x_token_major)
    consts = (_col(p["pre_norm"]), p["w_in"].T.astype(BF16), _col(p["q_norm"]), _col(p["k_norm"]))
    t_idx = jnp.arange(s)
    half = GQA_HD // 2
    tables = (_rope_tables((t_idx // GRID_W).astype(F32), half, GQA_ROPE_THETA)
              + _rope_tables((t_idx % GRID_W).astype(F32), half, GQA_ROPE_THETA))
    pre = _pre_call(_gqa_pre_kernel, "gqa_pre", x, x_token_major, consts, tables, GQA_HEADS,
                    GQA_KV_HEADS * GQA_HD // DK, GQA_KV_HEADS, GQA_HD)
    assert ATTN_STREAMS == GQA_GROUP
    o_t = _attn_call(_gqa_attn_kernel, "gqa_attn", pre, (), 1, lambda g: g // 2, 1)
    return _post_call(o_t, pre[3], x, x_token_major, y_token_major, p["w_out"].T.astype(BF16),
                      p["post_norm"])


def _diff_layer(x, p, layer_idx, x_token_major=False, y_token_major=False):
    lambda_init = 0.8 - 0.6 * math.exp(-0.3 * layer_idx)
    width = DIFF_HEADS * 2 * DIFF_HD
    w_in = p["w_in"]
    w_qvg = jnp.concatenate([w_in[:, 0:width], w_in[:, 2 * width:4 * width]], axis=1)
    consts = (_col(p["pre_norm"]), w_qvg.T.astype(BF16), w_in[:, width:2 * width].astype(BF16))
    pre = _pre_call(_diff_pre_kernel, "diff_pre", x, x_token_major, consts, (),
                    2 * DIFF_HEADS, DIFF_HEADS, DIFF_HEADS, 2 * DIFF_HD)
    row = lambda v: v.astype(F32).reshape(1, -1)
    extra = (row(p["lambda_q1"]), row(p["lambda_k1"]), row(p["lambda_q2"]), row(p["lambda_k2"]),
             _col(p["subln"]))
    o_t = _attn_call(functools.partial(_diff_attn_kernel, lambda_init=lambda_init), "diff_attn",
                     pre, extra, ATTN_STREAMS // 2, lambda g: g, ATTN_STREAMS // 2)
    return _post_call(o_t, pre[3], x, x_token_major, y_token_major, p["w_out"].T.astype(BF16),
                      p["post_norm"])


def _trunk(x, layers):
    last = len(layers) - 1
    for i, p in enumerate(layers):
        kind = i % 3
        layout = dict(x_token_major=(i == 0), y_token_major=(i == last))
        if kind == 0:
            x = _mla_layer(x, p, **layout)
        elif kind == 1:
            x = _gqa_layer(x, p, **layout)
        else:
            x = _diff_layer(x, p, i, **layout)
    return x


def kernel(x_prompt, x_sample, l0_pre_norm, l0_w_in, l0_q_norm, l0_w_uq, l0_kv_norm, l0_w_ukv, l0_w_out, l0_post_norm, l1_pre_norm, l1_w_in, l1_q_norm, l1_k_norm, l1_w_out, l1_post_norm, l2_pre_norm, l2_w_in, l2_lambda_q1, l2_lambda_k1, l2_lambda_q2, l2_lambda_k2, l2_subln, l2_w_out, l2_post_norm, l3_pre_norm, l3_w_in, l3_q_norm, l3_w_uq, l3_kv_norm, l3_w_ukv, l3_w_out, l3_post_norm):
    layers = [
        dict(pre_norm=l0_pre_norm, w_in=l0_w_in, q_norm=l0_q_norm, w_uq=l0_w_uq,
             kv_norm=l0_kv_norm, w_ukv=l0_w_ukv, w_out=l0_w_out, post_norm=l0_post_norm),
        dict(pre_norm=l1_pre_norm, w_in=l1_w_in, q_norm=l1_q_norm, k_norm=l1_k_norm,
             w_out=l1_w_out, post_norm=l1_post_norm),
        dict(pre_norm=l2_pre_norm, w_in=l2_w_in, lambda_q1=l2_lambda_q1, lambda_k1=l2_lambda_k1,
             lambda_q2=l2_lambda_q2, lambda_k2=l2_lambda_k2, subln=l2_subln,
             w_out=l2_w_out, post_norm=l2_post_norm),
        dict(pre_norm=l3_pre_norm, w_in=l3_w_in, q_norm=l3_q_norm, w_uq=l3_w_uq,
             kv_norm=l3_kv_norm, w_ukv=l3_w_ukv, w_out=l3_w_out, post_norm=l3_post_norm),
    ]
    return (_trunk(x_prompt, layers), _trunk(x_sample, layers))
```

```python
import functools
import math

import jax
import jax.numpy as jnp
from jax import lax
from jax.experimental import pallas as pl
from jax.experimental.pallas import tpu as pltpu

F32 = jnp.float32
BF16 = jnp.bfloat16

D_MODEL = 1024
NORM_EPS = 1e-6
GRID_W = 64

MLA_HEADS = 16
MLA_Q_LORA = 256
MLA_KV_LORA = 128
MLA_NOPE = 64
MLA_ROPE = 32
MLA_V = 64
MLA_QK = MLA_NOPE + MLA_ROPE
MLA_ROPE_THETA = 10000.0

GQA_HEADS = 16
GQA_KV_HEADS = 4
GQA_GROUP = GQA_HEADS // GQA_KV_HEADS
GQA_HD = 64
GQA_ROPE_THETA = 10000.0

DIFF_HEADS = 8
DIFF_HD = 64

LOG2_E = math.log2(math.e)

LANES = 128
BF16_ROWS = 16
DK = LANES
ONES_ROWS = BF16_ROWS
TOK_TILE = 512
Q_TILE = 512
ATTN_STREAMS = 4
SCORE_LIMIT = 60.0
SUBLANES = 8
MXU_ROWS = 256
MXU_COLS = 256
VMEM_LIMIT_BYTES = 56 * 1024 * 1024


def _rms_rows(x, gain_col):
    ms = jnp.mean(x * x, axis=0, keepdims=True)
    return x * lax.rsqrt(ms + NORM_EPS) * gain_col


def _rope_rows(x1, x2, cos, sin):
    return x1 * cos - x2 * sin, x1 * sin + x2 * cos


def _dot(a, b):
    return jnp.dot(a, b, preferred_element_type=F32)


def _feature_major(x_ref, token_major):
    return x_ref[0].T if token_major else x_ref[0]


def _store_norm_tile(worst, norm_out):
    top = jnp.max(jnp.max(worst, axis=0, keepdims=True), axis=1, keepdims=True)
    norm_out[0, 0] = jnp.broadcast_to(top, norm_out.shape[2:])


def _store_q_block(q_f32, q_out, head, worst):
    qb = q_f32.astype(BF16)
    q_out[0, head] = qb
    qf = qb.astype(F32)
    n2 = jnp.sum(qf * qf, axis=0, keepdims=True)
    return n2 if worst is None else jnp.maximum(worst, n2)


def _store_k_blocks(k_nat, k_out, knorm_out):
    worst = None
    for j in range(k_out.shape[1]):
        kb = k_nat[:, j * DK:(j + 1) * DK].astype(BF16)
        k_out[0, j] = kb
        kf = kb.astype(F32)
        n2 = jnp.sum(kf * kf, axis=1, keepdims=True)
        worst = n2 if worst is None else jnp.maximum(worst, n2)
    _store_norm_tile(worst, knorm_out)


def _mla_pre_kernel(x_ref, g_ref, win_ref, qn_ref, wuq_ref, kvn_ref, wv_ref, wk_ref,
                    cos_ref, sin_ref, q_out, k_out, v_out, gate_out, qnorm_out, knorm_out,
                    *, x_token_major):
    x = _feature_major(x_ref, x_token_major)
    t = x.shape[1]
    h = _rms_rows(x, g_ref[...]).astype(BF16)
    cos = cos_ref[...]
    sin = sin_ref[...]
    scale = MLA_QK ** -0.5 * LOG2_E

    cq = _dot(win_ref[0:MLA_Q_LORA, :], h)
    cqn = _rms_rows(cq, qn_ref[...]).astype(BF16)
    q_all = _dot(wuq_ref[...], cqn) * scale
    zeros_q = jnp.zeros((DK - MLA_QK, t), F32)
    half = MLA_ROPE // 2
    worst = None
    for hd in range(MLA_HEADS):
        base = hd * MLA_QK
        nope = q_all[base:base + MLA_NOPE]
        r1, r2 = _rope_rows(q_all[base + MLA_NOPE:base + MLA_NOPE + half],
                            q_all[base + MLA_NOPE + half:base + MLA_QK], cos, sin)
        worst = _store_q_block(jnp.concatenate([nope, r1, r2, zeros_q], axis=0), q_out, hd, worst)
    _store_norm_tile(worst, qnorm_out)

    kv_lo = MLA_Q_LORA
    kv_hi = MLA_Q_LORA + MLA_KV_LORA + MLA_ROPE
    ckr = _dot(win_ref[kv_lo:kv_hi, :], h)
    ckvn = _rms_rows(ckr[0:MLA_KV_LORA], kvn_ref[...])
    v_all = _dot(wv_ref[...], ckvn.astype(BF16))
    ones = jnp.ones((ONES_ROWS, t), BF16)
    for hd in range(MLA_HEADS):
        v_out[0, hd, 0, 0:MLA_V, :] = v_all[hd * MLA_V:(hd + 1) * MLA_V].astype(BF16)
        v_out[0, hd, 0, MLA_V:MLA_V + ONES_ROWS, :] = ones

    kr = ckr[MLA_KV_LORA:MLA_KV_LORA + MLA_ROPE]
    r1, r2 = _rope_rows(kr[0:half], kr[half:MLA_ROPE], cos, sin)
    krr = jnp.concatenate([r1, r2, jnp.zeros((LANES - MLA_ROPE, t), F32)], axis=0)
    z = jnp.concatenate([ckvn.T, krr.T], axis=1).astype(BF16)
    k_all = _dot(z, wk_ref[...])
    _store_k_blocks(k_all, k_out, knorm_out)

    gate_out[0] = _dot(win_ref[kv_hi:kv_hi + D_MODEL, :], h).astype(BF16)


def _gqa_pre_kernel(x_ref, g_ref, win_ref, qn_ref, kn_ref, cr_ref, sr_ref, cc_ref, sc_ref,
                    q_out, k_out, v_out, gate_out, qnorm_out, knorm_out, *, x_token_major):
    x = _feature_major(x_ref, x_token_major)
    t = x.shape[1]
    h = _rms_rows(x, g_ref[...]).astype(BF16)
    cr, sr, cc, sc = cr_ref[...], sr_ref[...], cc_ref[...], sc_ref[...]
    scale = GQA_HD ** -0.5 * LOG2_E
    n_q = GQA_HEADS * GQA_HD
    n_kv = GQA_KV_HEADS * GQA_HD
    qk = _dot(win_ref[0:n_q + n_kv, :], h)
    zeros_q = jnp.zeros((DK - GQA_HD, t), F32)
    quarter = GQA_HD // 4

    def norm_rope(blk, gain_col):
        xn = _rms_rows(blk, gain_col)
        a, b = _rope_rows(xn[0:quarter], xn[quarter:2 * quarter], cr, sr)
        c, d = _rope_rows(xn[2 * quarter:3 * quarter], xn[3 * quarter:4 * quarter], cc, sc)
        return jnp.concatenate([a, b, c, d], axis=0)

    worst = None
    for hd in range(GQA_HEADS):
        rq = norm_rope(qk[hd * GQA_HD:(hd + 1) * GQA_HD], qn_ref[...]) * scale
        kv = hd // GQA_GROUP
        parts = [rq, zeros_q] if kv % 2 == 0 else [zeros_q, rq]
        worst = _store_q_block(jnp.concatenate(parts, axis=0), q_out, hd, worst)
    _store_norm_tile(worst, qnorm_out)

    k_rows = [norm_rope(qk[n_q + kv * GQA_HD:n_q + (kv + 1) * GQA_HD], kn_ref[...])
              for kv in range(GQA_KV_HEADS)]
    k_nat = jnp.concatenate(k_rows, axis=0).T
    _store_k_blocks(k_nat, k_out, knorm_out)

    v = _dot(win_ref[n_q + n_kv:n_q + 2 * n_kv, :], h)
    ones = jnp.ones((ONES_ROWS, t), BF16)
    for kv in range(GQA_KV_HEADS):
        v_out[0, kv, 0, 0:GQA_HD, :] = v[kv * GQA_HD:(kv + 1) * GQA_HD].astype(BF16)
        v_out[0, kv, 0, GQA_HD:GQA_HD + ONES_ROWS, :] = ones

    g_lo = n_q + 2 * n_kv
    gate_out[0] = _dot(win_ref[g_lo:g_lo + D_MODEL, :], h).astype(BF16)


def _diff_pre_kernel(x_ref, g_ref, win_ref, wk_ref, q_out, k_out, v_out, gate_out, qnorm_out,
                     knorm_out, *, x_token_major):
    x = _feature_major(x_ref, x_token_major)
    t = x.shape[1]
    h = _rms_rows(x, g_ref[...]).astype(BF16)
    scale = DIFF_HD ** -0.5 * LOG2_E
    width = DIFF_HEADS * 2 * DIFF_HD
    q = _dot(win_ref[0:width, :], h) * scale
    zeros_q = jnp.zeros((DK - DIFF_HD, t), F32)
    worst = None
    for hc in range(2 * DIFF_HEADS):
        blk = q[hc * DIFF_HD:(hc + 1) * DIFF_HD]
        parts = [blk, zeros_q] if hc % 2 == 0 else [zeros_q, blk]
        worst = _store_q_block(jnp.concatenate(parts, axis=0), q_out, hc, worst)
    _store_norm_tile(worst, qnorm_out)

    k_nat = lax.dot_general(h, wk_ref[...], (((0,), (0,)), ((), ())),
                            preferred_element_type=F32)
    _store_k_blocks(k_nat, k_out, knorm_out)

    v = _dot(win_ref[width:2 * width, :], h)
    dv = 2 * DIFF_HD
    ones = jnp.ones((ONES_ROWS, t), BF16)
    for hd in range(DIFF_HEADS):
        v_out[0, hd, 0, 0:dv, :] = v[hd * dv:(hd + 1) * dv].astype(BF16)
        v_out[0, hd, 0, dv:dv + ONES_ROWS, :] = ones

    gate_out[0] = _dot(win_ref[2 * width:3 * width, :], h).astype(BF16)


def _attend(q_ts, k_chunk, v_chunk, nkc, dv, scratch, bias_fn=None):
    n = len(q_ts)
    tq = q_ts[0].shape[1]
    s_slots, p_slots = scratch[0:2], scratch[2:4]
    dvx = dv + ONES_ROWS
    assert nkc >= 2 and nkc % 2 == 0

    def qk(c, s_ref):
        maxima = []
        for g in range(n):
            s = _dot(k_chunk(g, c), q_ts[g])
            if bias_fn is not None:
                s = s + bias_fn(g, c)
            s_ref[g] = s
            maxima.append(jnp.max(s, axis=0, keepdims=True))
        return tuple(maxima)

    def softmax(s_ref, s_max, p_ref, m):
        m_out, alpha = [], []
        for g in range(n):
            m_new = jnp.maximum(m[g], s_max[g])
            alpha.append(jnp.exp2(m[g] - m_new))
            p_ref[g] = jnp.exp2(s_ref[g] - m_new).astype(BF16)
            m_out.append(m_new)
        return tuple(m_out), tuple(alpha)

    def pv(c, p_ref, alpha, acc):
        return tuple(alpha[g] * acc[g] + _dot(v_chunk(g, c), p_ref[g]) for g in range(n))

    m = tuple(jnp.full((1, tq), -jnp.inf, F32) for _ in range(n))
    acc = tuple(jnp.zeros((dvx, tq), F32) for _ in range(n))
    max0 = qk(0, s_slots[0])
    max1 = qk(1, s_slots[1])
    m, alpha = softmax(s_slots[0], max0, p_slots[0], m)

    def pair(j, carry):
        m, alpha, acc, max1 = carry
        t = 2 * j + 1
        max0 = qk(t + 1, s_slots[0])
        acc = pv(t - 1, p_slots[0], alpha, acc)
        m, alpha = softmax(s_slots[1], max1, p_slots[1], m)
        max1 = qk(t + 2, s_slots[1])
        acc = pv(t, p_slots[1], alpha, acc)
        m, alpha = softmax(s_slots[0], max0, p_slots[0], m)
        return m, alpha, acc, max1

    m, alpha, acc, max1 = lax.fori_loop(0, (nkc - 2) // 2, pair, (m, alpha, acc, max1))
    acc = pv(nkc - 2, p_slots[0], alpha, acc)
    m, alpha = softmax(s_slots[1], max1, p_slots[1], m)
    acc = pv(nkc - 1, p_slots[1], alpha, acc)
    return [a[0:dv] * (1.0 / a[dv:dv + 1]) for a in acc]


def _attend_bounded(q_ts, k_chunk, v_chunk, nkc, dv, scratch, bias_fn=None):
    n = len(q_ts)
    tq = q_ts[0].shape[1]
    p_slots = scratch[2:4]
    dvx = dv + ONES_ROWS
    assert nkc >= 2 and nkc % 2 == 0
    col_tiles = [slice(lo, lo + MXU_COLS) for lo in range(0, tq, MXU_COLS)]
    tk = scratch[2].shape[1]
    row_tiles = [slice(lo, lo + MXU_ROWS) for lo in range(0, tk, MXU_ROWS)]

    def stage(c, p_new, p_old, acc):
        ks = [None if p_new is None else k_chunk(g, c) for g in range(n)]
        vs = [None if p_old is None else v_chunk(g, c - 1) for g in range(n)]
        biases = [None if (bias_fn is None or p_new is None) else bias_fn(g, c)
                  for g in range(n)]
        parts = [[acc[g][:, cols] for cols in col_tiles] for g in range(n)]
        for rows in row_tiles:
            for ci, cols in enumerate(col_tiles):
                for g in range(n):
                    if p_new is not None:
                        s = _dot(ks[g][rows], q_ts[g][:, cols])
                        if biases[g] is not None:
                            s = s + biases[g][rows, cols]
                        p_new[g, rows, cols] = jnp.exp2(s).astype(BF16)
                    if p_old is not None:
                        parts[g][ci] = parts[g][ci] + _dot(vs[g][:, rows], p_old[g, rows, cols])
        return tuple(jnp.concatenate(parts[g], axis=1) for g in range(n))

    acc = tuple(jnp.zeros((dvx, tq), F32) for _ in range(n))
    acc = stage(0, p_slots[0], None, acc)

    def pair(j, acc):
        t = 2 * j + 1
        acc = stage(t, p_slots[1], p_slots[0], acc)
        return stage(t + 1, p_slots[0], p_slots[1], acc)

    acc = lax.fori_loop(0, (nkc - 2) // 2, pair, acc, unroll=True)
    acc = stage(nkc - 1, p_slots[1], p_slots[0], acc)
    acc = stage(nkc, None, p_slots[1], acc)
    return [a[0:dv] * (1.0 / a[dv:dv + 1]) for a in acc]


def _key_rows(c, tk):
    return pl.ds(pl.multiple_of(c * tk, tk), tk)


def _attend_dispatch(q_ts, k_chunk, v_chunk, norm_refs, nkc, dv, scratch, finish, bias_fn=None):
    qn_ref, kn_ref = norm_refs
    q_max2 = jnp.max(qn_ref[0, pl.program_id(2)])
    bounded = q_max2 * jnp.max(kn_ref[0]) <= SCORE_LIMIT * SCORE_LIMIT

    @pl.when(bounded)
    def _():
        finish(_attend_bounded(q_ts, k_chunk, v_chunk, nkc, dv, scratch, bias_fn))

    @pl.when(jnp.logical_not(bounded))
    def _():
        finish(_attend(q_ts, k_chunk, v_chunk, nkc, dv, scratch, bias_fn))


def _mla_attn_kernel(q_ref, k_ref, v_ref, qn2_ref, kn2_ref, o_ref, *scratch):
    nkc, tk = v_ref.shape[2], v_ref.shape[4]

    def finish(outs):
        for g, o in enumerate(outs):
            o_ref[0, g * MLA_V:(g + 1) * MLA_V, :] = o.astype(BF16)

    _attend_dispatch([q_ref[0, g] for g in range(ATTN_STREAMS)],
                     lambda g, c: k_ref[0, g, _key_rows(c, tk), :],
                     lambda g, c: v_ref[0, g, c], (qn2_ref, kn2_ref), nkc, MLA_V, scratch,
                     finish)


def _gqa_attn_kernel(q_ref, k_ref, v_ref, qn2_ref, kn2_ref, o_ref, *scratch):
    nkc, tk = v_ref.shape[2], v_ref.shape[4]

    def finish(outs):
        for g, o in enumerate(outs):
            o_ref[0, g * GQA_HD:(g + 1) * GQA_HD, :] = o.astype(BF16)

    _attend_dispatch([q_ref[0, g] for g in range(ATTN_STREAMS)],
                     lambda g, c: k_ref[0, 0, _key_rows(c, tk), :],
                     lambda g, c: v_ref[0, 0, c], (qn2_ref, kn2_ref), nkc, GQA_HD, scratch,
                     finish)


def _diff_attn_kernel(q_ref, k_ref, v_ref, qn2_ref, kn2_ref, lq1_ref, lk1_ref, lq2_ref, lk2_ref,
                      subln_ref, o_ref, *scratch, lambda_init):
    tq = q_ref.shape[3]
    nkc, tk = v_ref.shape[2], v_ref.shape[4]
    heads = ATTN_STREAMS // 2
    dv = 2 * DIFF_HD
    lam = (jnp.exp(jnp.sum(lq1_ref[...] * lk1_ref[...], axis=1, keepdims=True))
           - jnp.exp(jnp.sum(lq2_ref[...] * lk2_ref[...], axis=1, keepdims=True))
           + lambda_init)
    q_pos = (pl.program_id(2) * tq + lax.broadcasted_iota(jnp.int32, (1, tq), 1)).astype(F32)
    k_iota = lax.broadcasted_iota(jnp.int32, (tk, 1), 0)
    slopes = []
    for hd in range(heads):
        exponent = jnp.full((1, 1), 126 - (pl.program_id(1) * heads + hd), jnp.int32)
        slopes.append(lax.bitcast_convert_type(exponent << 23, F32) * LOG2_E)

    def bias_fn(g, c):
        k_pos = (c * tk + k_iota).astype(F32)
        return -slopes[g // 2] * jnp.abs(q_pos - k_pos)

    def finish(outs):
        for hd in range(heads):
            o = outs[2 * hd] - lam * outs[2 * hd + 1]
            o = _rms_rows(o, subln_ref[...]) * (1.0 - lambda_init)
            o_ref[0, hd * dv:(hd + 1) * dv, :] = o.astype(BF16)

    _attend_dispatch([q_ref[0, g] for g in range(ATTN_STREAMS)],
                     lambda g, c: k_ref[0, g // 2, _key_rows(c, tk), :],
                     lambda g, c: v_ref[0, g // 2, c], (qn2_ref, kn2_ref), nkc, dv, scratch, finish,
                     bias_fn)


def _post_kernel(o_ref, gate_ref, x_ref, wout_ref, g_ref, y_ref, *, x_token_major,
                 y_token_major):
    gate = gate_ref[0].astype(F32)
    og = (o_ref[0].astype(F32) * (gate / (1.0 + jnp.exp(-gate)))).astype(BF16)
    m = _dot(wout_ref[...], og)
    y = _feature_major(x_ref, x_token_major) + _rms_rows(m, g_ref[...])
    y_ref[0] = y.T if y_token_major else y


def _params(n_axes):
    return pltpu.CompilerParams(dimension_semantics=("arbitrary",) * n_axes,
                                vmem_limit_bytes=VMEM_LIMIT_BYTES)


def _const_spec(arr):
    nd = arr.ndim
    return pl.BlockSpec(arr.shape, lambda *_: (0,) * nd)


def _col(v):
    return v.astype(F32).reshape(-1, 1)


def _tiles(s):
    tok = min(TOK_TILE, s)
    tq = min(Q_TILE, s)
    assert s % tok == 0 and s % tq == 0
    return tok, tq


def _stream_dims(x, token_major):
    return (x.shape[0], x.shape[1]) if token_major else (x.shape[0], x.shape[2])


def _stream_spec(tok, token_major):
    if token_major:
        return pl.BlockSpec((1, tok, D_MODEL), lambda i, t: (i, t, 0))
    return pl.BlockSpec((1, D_MODEL, tok), lambda i, t: (i, 0, t))


def _pre_call(kernel, name, x, x_token_major, consts, tables, n_heads, n_kblocks, n_vheads, dv):
    b, s = _stream_dims(x, x_token_major)
    tok, _ = _tiles(s)
    dvx = dv + ONES_ROWS
    in_specs = ([_stream_spec(tok, x_token_major)]
                + [_const_spec(c) for c in consts]
                + [pl.BlockSpec((tb.shape[0], tok), lambda i, t: (0, t)) for tb in tables])
    norm_shape = jax.ShapeDtypeStruct((b, s // tok, SUBLANES, LANES), F32)
    norm_spec = pl.BlockSpec((1, 1, SUBLANES, LANES), lambda i, t: (i, t, 0, 0))
    out_shape = (jax.ShapeDtypeStruct((b, n_heads, DK, s), BF16),
                 jax.ShapeDtypeStruct((b, n_kblocks, s, DK), BF16),
                 jax.ShapeDtypeStruct((b, n_vheads, s // tok, dvx, tok), BF16),
                 jax.ShapeDtypeStruct((b, D_MODEL, s), BF16),
                 norm_shape, norm_shape)
    out_specs = (pl.BlockSpec((1, n_heads, DK, tok), lambda i, t: (i, 0, 0, t)),
                 pl.BlockSpec((1, n_kblocks, tok, DK), lambda i, t: (i, 0, t, 0)),
                 pl.BlockSpec((1, n_vheads, 1, dvx, tok), lambda i, t: (i, 0, t, 0, 0)),
                 pl.BlockSpec((1, D_MODEL, tok), lambda i, t: (i, 0, t)),
                 norm_spec, norm_spec)
    return pl.pallas_call(functools.partial(kernel, x_token_major=x_token_major),
                          out_shape=out_shape, grid=(b, s // tok), in_specs=in_specs,
                          out_specs=out_specs, compiler_params=_params(2), name=name)(
                              x, *consts, *tables)


def _attn_call(kernel, name, pre_outs, extra, k_blocks, k_block_of_group, v_heads):
    q_t, k, v_t, _, q_norm2, k_norm2 = pre_outs
    b, n_heads, _, s = q_t.shape
    tok, tq = _tiles(s)
    assert tq == tok
    nkc, dvx, tk = v_t.shape[2:]
    n_groups = n_heads // ATTN_STREAMS
    assert n_groups * v_heads == v_t.shape[1]
    in_specs = ([pl.BlockSpec((1, ATTN_STREAMS, DK, tq), lambda i, g, qi: (i, g, 0, qi)),
                 pl.BlockSpec((1, k_blocks, s, DK),
                              lambda i, g, qi: (i, k_block_of_group(g), 0, 0)),
                 pl.BlockSpec((1, v_heads, nkc, dvx, tk), lambda i, g, qi: (i, g, 0, 0, 0)),
                 pl.BlockSpec((1,) + q_norm2.shape[1:], lambda i, g, qi: (i, 0, 0, 0)),
                 pl.BlockSpec((1,) + k_norm2.shape[1:], lambda i, g, qi: (i, 0, 0, 0))]
                + [_const_spec(e) for e in extra])
    scratch = ([pltpu.VMEM((ATTN_STREAMS, tk, tq), F32)] * 2
               + [pltpu.VMEM((ATTN_STREAMS, tk, tq), BF16)] * 2)
    return pl.pallas_call(
        kernel, out_shape=jax.ShapeDtypeStruct((b, D_MODEL, s), BF16),
        grid=(b, n_groups, s // tq), in_specs=in_specs,
        out_specs=pl.BlockSpec((1, D_MODEL // n_groups, tq), lambda i, g, qi: (i, g, qi)),
        scratch_shapes=scratch, compiler_params=_params(3), name=name)(
            q_t, k, v_t, q_norm2, k_norm2, *extra)


def _post_call(o_t, gate_t, x, x_token_major, y_token_major, wout_t, post_gain):
    b, s = _stream_dims(x, x_token_major)
    tok, _ = _tiles(s)
    tile = _stream_spec(tok, False)
    consts = (wout_t, _col(post_gain))
    y_shape = (b, s, D_MODEL) if y_token_major else (b, D_MODEL, s)
    return pl.pallas_call(
        functools.partial(_post_kernel, x_token_major=x_token_major,
                          y_token_major=y_token_major),
        out_shape=jax.ShapeDtypeStruct(y_shape, F32), grid=(b, s // tok),
        in_specs=([tile, tile, _stream_spec(tok, x_token_major)]
                  + [_const_spec(c) for c in consts]),
        out_specs=_stream_spec(tok, y_token_major), compiler_params=_params(2), name="post")(
            o_t, gate_t, x, *consts)


def _rope_tables(pos, dim, theta):
    inv = 1.0 / (theta ** (jnp.arange(0, dim, 2, dtype=F32) / dim))
    ang = pos[:, None] * inv[None, :]
    return jnp.cos(ang).T, jnp.sin(ang).T


def _mla_layer(x, p, x_token_major=False, y_token_major=False):
    _, s = _stream_dims(x, x_token_major)
    w_ukv = p["w_ukv"].reshape(MLA_KV_LORA, MLA_HEADS, MLA_NOPE + MLA_V)
    wv_t = w_ukv[:, :, MLA_NOPE:].reshape(MLA_KV_LORA, MLA_HEADS * MLA_V).T
    wk = jnp.zeros((2 * LANES, MLA_HEADS, DK), F32)
    wk = wk.at[0:MLA_KV_LORA, :, 0:MLA_NOPE].set(w_ukv[:, :, 0:MLA_NOPE])
    eye = jnp.broadcast_to(jnp.eye(MLA_ROPE, dtype=F32)[:, None, :], (MLA_ROPE, MLA_HEADS, MLA_ROPE))
    wk = wk.at[LANES:LANES + MLA_ROPE, :, MLA_NOPE:MLA_QK].set(eye)
    consts = (_col(p["pre_norm"]), p["w_in"].T.astype(BF16), _col(p["q_norm"]),
              p["w_uq"].T.astype(BF16), _col(p["kv_norm"]), wv_t.astype(BF16),
              wk.reshape(2 * LANES, MLA_HEADS * DK).astype(BF16))
    tables = _rope_tables(jnp.arange(s, dtype=F32), MLA_ROPE, MLA_ROPE_THETA)
    pre = _pre_call(_mla_pre_kernel, "mla_pre", x, x_token_major, consts, tables,
                    MLA_HEADS, MLA_HEADS, MLA_HEADS, MLA_V)
    o_t = _attn_call(_mla_attn_kernel, "mla_attn", pre, (), ATTN_STREAMS, lambda g: g,
                     ATTN_STREAMS)
    return _post_call(o_t, pre[3], x, x_token_major, y_token_major, p["w_out"].T.astype(BF16),
                      p["post_norm"])


def _gqa_layer(x, p, x_token_major=False, y_token_major=False):
    _, s = _stream_dims(x, x_token_major)
    consts = (_col(p["pre_norm"]), p["w_in"].T.astype(BF16), _col(p["q_norm"]), _col(p["k_norm"]))
    t_idx = jnp.arange(s)
    half = GQA_HD // 2
    tables = (_rope_tables((t_idx // GRID_W).astype(F32), half, GQA_ROPE_THETA)
              + _rope_tables((t_idx % GRID_W).astype(F32), half, GQA_ROPE_THETA))
    pre = _pre_call(_gqa_pre_kernel, "gqa_pre", x, x_token_major, consts, tables, GQA_HEADS,
                    GQA_KV_HEADS * GQA_HD // DK, GQA_KV_HEADS, GQA_HD)
    assert ATTN_STREAMS == GQA_GROUP
    o_t = _attn_call(_gqa_attn_kernel, "gqa_attn", pre, (), 1, lambda g: g // 2, 1)
    return _post_call(o_t, pre[3], x, x_token_major, y_token_major, p["w_out"].T.astype(BF16),
                      p["post_norm"])


def _diff_layer(x, p, layer_idx, x_token_major=False, y_token_major=False):
    lambda_init = 0.8 - 0.6 * math.exp(-0.3 * layer_idx)
    width = DIFF_HEADS * 2 * DIFF_HD
    w_in = p["w_in"]
    w_qvg = jnp.concatenate([w_in[:, 0:width], w_in[:, 2 * width:4 * width]], axis=1)
    consts = (_col(p["pre_norm"]), w_qvg.T.astype(BF16), w_in[:, width:2 * width].astype(BF16))
    pre = _pre_call(_diff_pre_kernel, "diff_pre", x, x_token_major, consts, (),
                    2 * DIFF_HEADS, DIFF_HEADS, DIFF_HEADS, 2 * DIFF_HD)
    row = lambda v: v.astype(F32).reshape(1, -1)
    extra = (row(p["lambda_q1"]), row(p["lambda_k1"]), row(p["lambda_q2"]), row(p["lambda_k2"]),
             _col(p["subln"]))
    o_t = _attn_call(functools.partial(_diff_attn_kernel, lambda_init=lambda_init), "diff_attn",
                     pre, extra, ATTN_STREAMS // 2, lambda g: g, ATTN_STREAMS // 2)
    return _post_call(o_t, pre[3], x, x_token_major, y_token_major, p["w_out"].T.astype(BF16),
                      p["post_norm"])


def _trunk(x, layers):
    last = len(layers) - 1
    for i, p in enumerate(layers):
        kind = i % 3
        layout = dict(x_token_major=(i == 0), y_token_major=(i == last))
        if kind == 0:
            x = _mla_layer(x, p, **layout)
        elif kind == 1:
            x = _gqa_layer(x, p, **layout)
        else:
            x = _diff_layer(x, p, i, **layout)
    return x


def kernel(x_prompt, x_sample, l0_pre_norm, l0_w_in, l0_q_norm, l0_w_uq, l0_kv_norm, l0_w_ukv, l0_w_out, l0_post_norm, l1_pre_norm, l1_w_in, l1_q_norm, l1_k_norm, l1_w_out, l1_post_norm, l2_pre_norm, l2_w_in, l2_lambda_q1, l2_lambda_k1, l2_lambda_q2, l2_lambda_k2, l2_subln, l2_w_out, l2_post_norm, l3_pre_norm, l3_w_in, l3_q_norm, l3_w_uq, l3_kv_norm, l3_w_ukv, l3_w_out, l3_post_norm):
    layers = [
        dict(pre_norm=l0_pre_norm, w_in=l0_w_in, q_norm=l0_q_norm, w_uq=l0_w_uq,
             kv_norm=l0_kv_norm, w_ukv=l0_w_ukv, w_out=l0_w_out, post_norm=l0_post_norm),
        dict(pre_norm=l1_pre_norm, w_in=l1_w_in, q_norm=l1_q_norm, k_norm=l1_k_norm,
             w_out=l1_w_out, post_norm=l1_post_norm),
        dict(pre_norm=l2_pre_norm, w_in=l2_w_in, lambda_q1=l2_lambda_q1, lambda_k1=l2_lambda_k1,
             lambda_q2=l2_lambda_q2, lambda_k2=l2_lambda_k2, subln=l2_subln,
             w_out=l2_w_out, post_norm=l2_post_norm),
        dict(pre_norm=l3_pre_norm, w_in=l3_w_in, q_norm=l3_q_norm, w_uq=l3_w_uq,
             kv_norm=l3_kv_norm, w_ukv=l3_w_ukv, w_out=l3_w_out, post_norm=l3_post_norm),
    ]
    return (_trunk(x_prompt, layers), _trunk(x_sample, layers))
```

```python
import functools
import math

import jax
import jax.numpy as jnp
from jax import lax
from jax.experimental import pallas as pl
from jax.experimental.pallas import tpu as pltpu

F32 = jnp.float32
BF16 = jnp.bfloat16

D_MODEL = 1024
NORM_EPS = 1e-6
GRID_W = 64

MLA_HEADS = 16
MLA_Q_LORA = 256
MLA_KV_LORA = 128
MLA_NOPE = 64
MLA_ROPE = 32
MLA_V = 64
MLA_QK = MLA_NOPE + MLA_ROPE
MLA_ROPE_THETA = 10000.0

GQA_HEADS = 16
GQA_KV_HEADS = 4
GQA_GROUP = GQA_HEADS // GQA_KV_HEADS
GQA_HD = 64
GQA_ROPE_THETA = 10000.0

DIFF_HEADS = 8
DIFF_HD = 64

LOG2_E = math.log2(math.e)

LANES = 128
BF16_ROWS = 16
DK = LANES
ONES_ROWS = BF16_ROWS
TOK_TILE = 512
Q_TILE = 512
ATTN_STREAMS = 4
SCORE_LIMIT = 60.0
SUBLANES = 8
MXU_ROWS = 256
MXU_COLS = 256
VMEM_LIMIT_BYTES = 56 * 1024 * 1024


def _rms_rows(x, gain_col):
    ms = jnp.mean(x * x, axis=0, keepdims=True)
    return x * lax.rsqrt(ms + NORM_EPS) * gain_col


def _rope_rows(x1, x2, cos, sin):
    return x1 * cos - x2 * sin, x1 * sin + x2 * cos


def _dot(a, b):
    return jnp.dot(a, b, preferred_element_type=F32)


def _feature_major(x_ref, token_major):
    return x_ref[0].T if token_major else x_ref[0]


def _store_norm_tile(worst, norm_out):
    top = jnp.max(jnp.max(worst, axis=0, keepdims=True), axis=1, keepdims=True)
    norm_out[0, 0] = jnp.broadcast_to(top, norm_out.shape[2:])


def _store_q_block(q_f32, q_out, head, worst, norm_rows=DK):
    qb = q_f32.astype(BF16)
    q_out[0, head] = qb
    qf = qb[0:norm_rows].astype(F32)
    n2 = jnp.sum(qf * qf, axis=0, keepdims=True)
    return n2 if worst is None else jnp.maximum(worst, n2)


def _store_k_blocks(k_nat, k_out, knorm_out, extras=None):
    worst = None
    for j in range(k_out.shape[1]):
        kb = k_nat[:, j * DK:(j + 1) * DK].astype(BF16)
        kf = kb.astype(F32)
        n2 = jnp.sum(kf * kf, axis=1, keepdims=True)
        worst = n2 if worst is None else jnp.maximum(worst, n2)
        k_out[0, j] = kb if extras is None else (kf + extras(j)).astype(BF16)
    _store_norm_tile(worst, knorm_out)


def _split3(x):
    hi = x.astype(BF16).astype(F32)
    rest = x - hi
    mid = rest.astype(BF16).astype(F32)
    lo = (rest - mid).astype(BF16).astype(F32)
    return hi, mid, lo


def _alibi_block(offset, value, terms_at):
    hi, mid, lo = _split3(value)
    rel = offset - terms_at
    terms = jnp.where(rel == 0, hi, jnp.where(rel == 1, mid, lo))
    is_term = (rel >= 0) & (rel < 3)
    in_block = (offset >= 0) & (offset < 6)
    return jnp.where(is_term, terms, jnp.where(in_block, 1.0, 0.0))


def _mla_pre_kernel(x_ref, g_ref, win_ref, qn_ref, wuq_ref, kvn_ref, wv_ref, wk_ref,
                    cos_ref, sin_ref, q_out, k_out, v_out, gate_out, qnorm_out, knorm_out,
                    *, x_token_major):
    x = _feature_major(x_ref, x_token_major)
    t = x.shape[1]
    h = _rms_rows(x, g_ref[...]).astype(BF16)
    cos = cos_ref[...]
    sin = sin_ref[...]
    scale = MLA_QK ** -0.5 * LOG2_E

    cq = _dot(win_ref[0:MLA_Q_LORA, :], h)
    cqn = _rms_rows(cq, qn_ref[...]).astype(BF16)
    q_all = _dot(wuq_ref[...], cqn) * scale
    zeros_q = jnp.zeros((DK - MLA_QK, t), F32)
    half = MLA_ROPE // 2
    worst = None
    for hd in range(MLA_HEADS):
        base = hd * MLA_QK
        nope = q_all[base:base + MLA_NOPE]
        r1, r2 = _rope_rows(q_all[base + MLA_NOPE:base + MLA_NOPE + half],
                            q_all[base + MLA_NOPE + half:base + MLA_QK], cos, sin)
        worst = _store_q_block(jnp.concatenate([nope, r1, r2, zeros_q], axis=0), q_out, hd, worst)
    _store_norm_tile(worst, qnorm_out)

    kv_lo = MLA_Q_LORA
    kv_hi = MLA_Q_LORA + MLA_KV_LORA + MLA_ROPE
    ckr = _dot(win_ref[kv_lo:kv_hi, :], h)
    ckvn = _rms_rows(ckr[0:MLA_KV_LORA], kvn_ref[...])
    v_all = _dot(wv_ref[...], ckvn.astype(BF16))
    ones = jnp.ones((ONES_ROWS, t), BF16)
    for hd in range(MLA_HEADS):
        v_out[0, hd, 0, 0:MLA_V, :] = v_all[hd * MLA_V:(hd + 1) * MLA_V].astype(BF16)
        v_out[0, hd, 0, MLA_V:MLA_V + ONES_ROWS, :] = ones

    kr = ckr[MLA_KV_LORA:MLA_KV_LORA + MLA_ROPE]
    r1, r2 = _rope_rows(kr[0:half], kr[half:MLA_ROPE], cos, sin)
    krr = jnp.concatenate([r1, r2, jnp.zeros((LANES - MLA_ROPE, t), F32)], axis=0)
    z = jnp.concatenate([ckvn.T, krr.T], axis=1).astype(BF16)
    k_all = _dot(z, wk_ref[...])
    _store_k_blocks(k_all, k_out, knorm_out)

    gate_out[0] = _dot(win_ref[kv_hi:kv_hi + D_MODEL, :], h).astype(BF16)


def _gqa_pre_kernel(x_ref, g_ref, win_ref, qn_ref, kn_ref, cr_ref, sr_ref, cc_ref, sc_ref,
                    q_out, k_out, v_out, gate_out, qnorm_out, knorm_out, *, x_token_major):
    x = _feature_major(x_ref, x_token_major)
    t = x.shape[1]
    h = _rms_rows(x, g_ref[...]).astype(BF16)
    cr, sr, cc, sc = cr_ref[...], sr_ref[...], cc_ref[...], sc_ref[...]
    scale = GQA_HD ** -0.5 * LOG2_E
    n_q = GQA_HEADS * GQA_HD
    n_kv = GQA_KV_HEADS * GQA_HD
    qk = _dot(win_ref[0:n_q + n_kv, :], h)
    zeros_q = jnp.zeros((DK - GQA_HD, t), F32)
    quarter = GQA_HD // 4

    def norm_rope(blk, gain_col):
        xn = _rms_rows(blk, gain_col)
        a, b = _rope_rows(xn[0:quarter], xn[quarter:2 * quarter], cr, sr)
        c, d = _rope_rows(xn[2 * quarter:3 * quarter], xn[3 * quarter:4 * quarter], cc, sc)
        return jnp.concatenate([a, b, c, d], axis=0)

    worst = None
    for hd in range(GQA_HEADS):
        rq = norm_rope(qk[hd * GQA_HD:(hd + 1) * GQA_HD], qn_ref[...]) * scale
        kv = hd // GQA_GROUP
        parts = [rq, zeros_q] if kv % 2 == 0 else [zeros_q, rq]
        worst = _store_q_block(jnp.concatenate(parts, axis=0), q_out, hd, worst)
    _store_norm_tile(worst, qnorm_out)

    k_rows = [norm_rope(qk[n_q + kv * GQA_HD:n_q + (kv + 1) * GQA_HD], kn_ref[...])
              for kv in range(GQA_KV_HEADS)]
    k_nat = jnp.concatenate(k_rows, axis=0).T
    _store_k_blocks(k_nat, k_out, knorm_out)

    v = _dot(win_ref[n_q + n_kv:n_q + 2 * n_kv, :], h)
    ones = jnp.ones((ONES_ROWS, t), BF16)
    for kv in range(GQA_KV_HEADS):
        v_out[0, kv, 0, 0:GQA_HD, :] = v[kv * GQA_HD:(kv + 1) * GQA_HD].astype(BF16)
        v_out[0, kv, 0, GQA_HD:GQA_HD + ONES_ROWS, :] = ones

    g_lo = n_q + 2 * n_kv
    gate_out[0] = _dot(win_ref[g_lo:g_lo + D_MODEL, :], h).astype(BF16)


def _diff_pre_kernel(x_ref, g_ref, win_ref, wk_ref, q_out, k_out, v_out, gate_out, qnorm_out,
                     knorm_out, *, x_token_major):
    x = _feature_major(x_ref, x_token_major)
    t = x.shape[1]
    h = _rms_rows(x, g_ref[...]).astype(BF16)
    scale = DIFF_HD ** -0.5 * LOG2_E
    width = DIFF_HEADS * 2 * DIFF_HD
    q = _dot(win_ref[0:width, :], h) * scale
    first = pl.program_id(1) * t
    pos_row = (first + lax.broadcasted_iota(jnp.int32, (1, t), 1)).astype(F32)
    pos_col = (first + lax.broadcasted_iota(jnp.int32, (t, 1), 0)).astype(F32)
    q_offset = lax.broadcasted_iota(jnp.int32, (BF16_ROWS, t), 0)
    k_offset = lax.broadcasted_iota(jnp.int32, (t, DK), 1) - DIFF_HD
    slopes = [2.0 ** -(hd + 1) * LOG2_E for hd in range(DIFF_HEADS)]
    zeros_q = jnp.zeros((DK - DIFF_HD - BF16_ROWS, t), F32)
    worst = None
    for hc in range(2 * DIFF_HEADS):
        bias_rows = _alibi_block(q_offset, -slopes[hc // 2] * pos_row, 0)
        blk = jnp.concatenate([q[hc * DIFF_HD:(hc + 1) * DIFF_HD], bias_rows, zeros_q], axis=0)
        worst = _store_q_block(blk, q_out, hc, worst, norm_rows=DIFF_HD)
    _store_norm_tile(worst, qnorm_out)

    k_nat = lax.dot_general(h, wk_ref[...], (((0,), (0,)), ((), ())),
                            preferred_element_type=F32)
    _store_k_blocks(k_nat, k_out, knorm_out,
                    lambda hc: _alibi_block(k_offset, slopes[hc // 2] * pos_col, 3))

    v = _dot(win_ref[width:2 * width, :], h)
    dv = 2 * DIFF_HD
    ones = jnp.ones((ONES_ROWS, t), BF16)
    for hd in range(DIFF_HEADS):
        v_out[0, hd, 0, 0:dv, :] = v[hd * dv:(hd + 1) * dv].astype(BF16)
        v_out[0, hd, 0, dv:dv + ONES_ROWS, :] = ones

    gate_out[0] = _dot(win_ref[2 * width:3 * width, :], h).astype(BF16)


def _attend(q_ts, k_chunk, v_chunk, nkc, dv, scratch, bias_fn=None):
    n = len(q_ts)
    tq = q_ts[0].shape[1]
    s_slots, p_slots = scratch[0:2], scratch[2:4]
    dvx = dv + ONES_ROWS
    assert nkc >= 2 and nkc % 2 == 0

    def qk(c, s_ref):
        maxima = []
        for g in range(n):
            s = _dot(k_chunk(g, c), q_ts[g])
            if bias_fn is not None:
                s = s + bias_fn(g, c)
            s_ref[g] = s
            maxima.append(jnp.max(s, axis=0, keepdims=True))
        return tuple(maxima)

    def softmax(s_ref, s_max, p_ref, m):
        m_out, alpha = [], []
        for g in range(n):
            m_new = jnp.maximum(m[g], s_max[g])
            alpha.append(jnp.exp2(m[g] - m_new))
            p_ref[g] = jnp.exp2(s_ref[g] - m_new).astype(BF16)
            m_out.append(m_new)
        return tuple(m_out), tuple(alpha)

    def pv(c, p_ref, alpha, acc):
        return tuple(alpha[g] * acc[g] + _dot(v_chunk(g, c), p_ref[g]) for g in range(n))

    m = tuple(jnp.full((1, tq), -jnp.inf, F32) for _ in range(n))
    acc = tuple(jnp.zeros((dvx, tq), F32) for _ in range(n))
    max0 = qk(0, s_slots[0])
    max1 = qk(1, s_slots[1])
    m, alpha = softmax(s_slots[0], max0, p_slots[0], m)

    def pair(j, carry):
        m, alpha, acc, max1 = carry
        t = 2 * j + 1
        max0 = qk(t + 1, s_slots[0])
        acc = pv(t - 1, p_slots[0], alpha, acc)
        m, alpha = softmax(s_slots[1], max1, p_slots[1], m)
        max1 = qk(t + 2, s_slots[1])
        acc = pv(t, p_slots[1], alpha, acc)
        m, alpha = softmax(s_slots[0], max0, p_slots[0], m)
        return m, alpha, acc, max1

    m, alpha, acc, max1 = lax.fori_loop(0, (nkc - 2) // 2, pair, (m, alpha, acc, max1))
    acc = pv(nkc - 2, p_slots[0], alpha, acc)
    m, alpha = softmax(s_slots[1], max1, p_slots[1], m)
    acc = pv(nkc - 1, p_slots[1], alpha, acc)
    return [a[0:dv] * (1.0 / a[dv:dv + 1]) for a in acc]


def _attend_bounded(q_of, k_chunk, v_chunk, n, nkc, dv, scratch, chunk_of=None, first_bias=None):
    p_slots = scratch[2:4]
    tk, tq = p_slots[0].shape[1:]
    dvx = dv + ONES_ROWS
    col_tiles = [slice(lo, lo + MXU_COLS) for lo in range(0, tq, MXU_COLS)]
    row_tiles = [slice(lo, lo + MXU_ROWS) for lo in range(0, tk, MXU_ROWS)]
    if chunk_of is None:
        chunk_of = lambda t: t

    def stage(t, acc, den):
        p_new = p_slots[t % 2] if t < nkc else None
        p_old = p_slots[(t - 1) % 2] if t > 0 else None
        qs = [None if p_new is None else q_of(g, t) for g in range(n)]
        ks = [None if p_new is None else k_chunk(g, chunk_of(t)) for g in range(n)]
        vs = [None if p_old is None else v_chunk(g, chunk_of(t - 1))[0:dv] for g in range(n)]
        biases = [first_bias(g) if (first_bias is not None and t == 0) else None
                  for g in range(n)]
        parts = [[acc[g][:, cols] for cols in col_tiles] for g in range(n)]
        dens = [[den[g][:, cols] for cols in col_tiles] for g in range(n)]
        for rows in row_tiles:
            for ci, cols in enumerate(col_tiles):
                for g in range(n):
                    if p_new is not None:
                        s = _dot(ks[g][rows], qs[g][:, cols])
                        if biases[g] is not None:
                            s = s + biases[g][rows, cols]
                        p = jnp.exp2(s)
                        p_new[g, rows, cols] = p.astype(BF16)
                        dens[g][ci] = dens[g][ci] + jnp.sum(
                            p.reshape(-1, SUBLANES, MXU_COLS), axis=0)
                    if p_old is not None:
                        parts[g][ci] = parts[g][ci] + _dot(vs[g][:, rows], p_old[g, rows, cols])
        return (tuple(jnp.concatenate(parts[g], axis=1) for g in range(n)),
                tuple(jnp.concatenate(dens[g], axis=1) for g in range(n)))

    acc = tuple(jnp.zeros((dv, tq), F32) for _ in range(n))
    den = tuple(jnp.zeros((SUBLANES, tq), F32) for _ in range(n))
    for t in range(nkc + 1):
        acc, den = stage(t, acc, den)
    return [a * (1.0 / jnp.sum(d, axis=0, keepdims=True)) for a, d in zip(acc, den)]


def _key_rows(c, tk):
    if isinstance(c, int):
        return pl.ds(c * tk, tk)
    return pl.ds(pl.multiple_of(c * tk, tk), tk)


def _attend_dispatch(q_ts, k_chunk, v_chunk, norm_refs, nkc, dv, scratch, finish, bias_fn=None,
                     bounded_plan=None):
    assert (bias_fn is None) == (bounded_plan is None)
    qn_ref, kn_ref = norm_refs
    q_max2 = jnp.max(qn_ref[0, pl.program_id(2)])
    bounded = q_max2 * jnp.max(kn_ref[0]) <= SCORE_LIMIT * SCORE_LIMIT
    q_of, chunk_of, first_bias = bounded_plan or (lambda g, t: q_ts[g], None, None)

    @pl.when(bounded)
    def _():
        finish(_attend_bounded(q_of, k_chunk, v_chunk, len(q_ts), nkc, dv, scratch, chunk_of,
                               first_bias))

    @pl.when(jnp.logical_not(bounded))
    def _():
        finish(_attend(q_ts, k_chunk, v_chunk, nkc, dv, scratch, bias_fn))


def _mla_attn_kernel(q_ref, k_ref, v_ref, qn2_ref, kn2_ref, o_ref, *scratch):
    nkc, tk = v_ref.shape[2], v_ref.shape[4]

    def finish(outs):
        for g, o in enumerate(outs):
            o_ref[0, g * MLA_V:(g + 1) * MLA_V, :] = o.astype(BF16)

    _attend_dispatch([q_ref[0, g] for g in range(ATTN_STREAMS)],
                     lambda g, c: k_ref[0, g, _key_rows(c, tk), :],
                     lambda g, c: v_ref[0, g, c], (qn2_ref, kn2_ref), nkc, MLA_V, scratch,
                     finish)


def _gqa_attn_kernel(q_ref, k_ref, v_ref, qn2_ref, kn2_ref, o_ref, *scratch):
    nkc, tk = v_ref.shape[2], v_ref.shape[4]

    def finish(outs):
        for g, o in enumerate(outs):
            o_ref[0, g * GQA_HD:(g + 1) * GQA_HD, :] = o.astype(BF16)

    _attend_dispatch([q_ref[0, g] for g in range(ATTN_STREAMS)],
                     lambda g, c: k_ref[0, 0, _key_rows(c, tk), :],
                     lambda g, c: v_ref[0, 0, c], (qn2_ref, kn2_ref), nkc, GQA_HD, scratch,
                     finish)


def _diff_attn_kernel(q_ref, k_ref, v_ref, qn2_ref, kn2_ref, lq1_ref, lk1_ref, lq2_ref, lk2_ref,
                      subln_ref, o_ref, *scratch, lambda_init):
    tq = q_ref.shape[3]
    nkc, tk = v_ref.shape[2], v_ref.shape[4]
    heads = ATTN_STREAMS // 2
    dv = 2 * DIFF_HD
    lam = (jnp.exp(jnp.sum(lq1_ref[...] * lk1_ref[...], axis=1, keepdims=True))
           - jnp.exp(jnp.sum(lq2_ref[...] * lk2_ref[...], axis=1, keepdims=True))
           + lambda_init)
    assert tq == tk
    qi = pl.program_id(2)
    q_iota = lax.broadcasted_iota(jnp.int32, (1, tq), 1)
    k_iota = lax.broadcasted_iota(jnp.int32, (tk, 1), 0)
    q_pos = (qi * tq + q_iota).astype(F32)
    slopes = []
    for hd in range(heads):
        exponent = jnp.full((1, 1), 126 - (pl.program_id(1) * heads + hd), jnp.int32)
        slopes.append(lax.bitcast_convert_type(exponent << 23, F32) * LOG2_E)

    def bias_fn(g, c):
        k_pos = (c * tk + k_iota).astype(F32)
        return -slopes[g // 2] * jnp.abs(q_pos - k_pos)

    lo, hi = DIFF_HD, DIFF_HD + BF16_ROWS
    q_all = [q_ref[0, g] for g in range(ATTN_STREAMS)]
    terms = [q[lo:hi].astype(F32) for q in q_all]

    def with_terms(g, sign):
        return jnp.concatenate([q_all[g][0:lo], (terms[g] * sign).astype(BF16), q_all[g][hi:DK]],
                               axis=0)

    q_plain = [with_terms(g, 0.0) for g in range(ATTN_STREAMS)]

    def chunk_of(t):
        return qi if t == 0 else lax.rem(qi + t, nkc)

    def q_of(g, t):
        if t == 0:
            return q_plain[g]
        wrapped = qi + t >= nkc
        return with_terms(g, jnp.where(wrapped, 1.0, -1.0))

    def first_bias(g):
        return -slopes[g // 2] * jnp.abs((q_iota - k_iota).astype(F32))

    def finish(outs):
        for hd in range(heads):
            o = outs[2 * hd] - lam * outs[2 * hd + 1]
            o = _rms_rows(o, subln_ref[...]) * (1.0 - lambda_init)
            o_ref[0, hd * dv:(hd + 1) * dv, :] = o.astype(BF16)

    _attend_dispatch(q_plain, lambda g, c: k_ref[0, g, _key_rows(c, tk), :],
                     lambda g, c: v_ref[0, g // 2, c], (qn2_ref, kn2_ref), nkc, dv, scratch, finish,
                     bias_fn, (q_of, chunk_of, first_bias))


def _post_kernel(o_ref, gate_ref, x_ref, wout_ref, g_ref, y_ref, *, x_token_major,
                 y_token_major):
    gate = gate_ref[0].astype(F32)
    og = (o_ref[0].astype(F32) * (gate / (1.0 + jnp.exp(-gate)))).astype(BF16)
    m = _dot(wout_ref[...], og)
    y = _feature_major(x_ref, x_token_major) + _rms_rows(m, g_ref[...])
    y_ref[0] = y.T if y_token_major else y


def _params(n_axes):
    return pltpu.CompilerParams(dimension_semantics=("arbitrary",) * n_axes,
                                vmem_limit_bytes=VMEM_LIMIT_BYTES)


def _const_spec(arr):
    nd = arr.ndim
    return pl.BlockSpec(arr.shape, lambda *_: (0,) * nd)


def _col(v):
    return v.astype(F32).reshape(-1, 1)


def _tiles(s):
    tok = min(TOK_TILE, s)
    tq = min(Q_TILE, s)
    assert s % tok == 0 and s % tq == 0
    return tok, tq


def _stream_dims(x, token_major):
    return (x.shape[0], x.shape[1]) if token_major else (x.shape[0], x.shape[2])


def _stream_spec(tok, token_major):
    if token_major:
        return pl.BlockSpec((1, tok, D_MODEL), lambda i, t: (i, t, 0))
    return pl.BlockSpec((1, D_MODEL, tok), lambda i, t: (i, 0, t))


def _pre_call(kernel, name, x, x_token_major, consts, tables, n_heads, n_kblocks, n_vheads, dv):
    b, s = _stream_dims(x, x_token_major)
    tok, _ = _tiles(s)
    dvx = dv + ONES_ROWS
    in_specs = ([_stream_spec(tok, x_token_major)]
                + [_const_spec(c) for c in consts]
                + [pl.BlockSpec((tb.shape[0], tok), lambda i, t: (0, t)) for tb in tables])
    norm_shape = jax.ShapeDtypeStruct((b, s // tok, SUBLANES, LANES), F32)
    norm_spec = pl.BlockSpec((1, 1, SUBLANES, LANES), lambda i, t: (i, t, 0, 0))
    out_shape = (jax.ShapeDtypeStruct((b, n_heads, DK, s), BF16),
                 jax.ShapeDtypeStruct((b, n_kblocks, s, DK), BF16),
                 jax.ShapeDtypeStruct((b, n_vheads, s // tok, dvx, tok), BF16),
                 jax.ShapeDtypeStruct((b, D_MODEL, s), BF16),
                 norm_shape, norm_shape)
    out_specs = (pl.BlockSpec((1, n_heads, DK, tok), lambda i, t: (i, 0, 0, t)),
                 pl.BlockSpec((1, n_kblocks, tok, DK), lambda i, t: (i, 0, t, 0)),
                 pl.BlockSpec((1, n_vheads, 1, dvx, tok), lambda i, t: (i, 0, t, 0, 0)),
                 pl.BlockSpec((1, D_MODEL, tok), lambda i, t: (i, 0, t)),
                 norm_spec, norm_spec)
    return pl.pallas_call(functools.partial(kernel, x_token_major=x_token_major),
                          out_shape=out_shape, grid=(b, s // tok), in_specs=in_specs,
                          out_specs=out_specs, compiler_params=_params(2), name=name)(
                              x, *consts, *tables)


def _attn_call(kernel, name, pre_outs, extra, k_blocks, k_block_of_group, v_heads):
    q_t, k, v_t, _, q_norm2, k_norm2 = pre_outs
    b, n_heads, _, s = q_t.shape
    tok, tq = _tiles(s)
    assert tq == tok
    nkc, dvx, tk = v_t.shape[2:]
    n_groups = n_heads // ATTN_STREAMS
    assert n_groups * v_heads == v_t.shape[1]
    in_specs = ([pl.BlockSpec((1, ATTN_STREAMS, DK, tq), lambda i, g, qi: (i, g, 0, qi)),
                 pl.BlockSpec((1, k_blocks, s, DK),
                              lambda i, g, qi: (i, k_block_of_group(g), 0, 0)),
                 pl.BlockSpec((1, v_heads, nkc, dvx, tk), lambda i, g, qi: (i, g, 0, 0, 0)),
                 pl.BlockSpec((1,) + q_norm2.shape[1:], lambda i, g, qi: (i, 0, 0, 0)),
                 pl.BlockSpec((1,) + k_norm2.shape[1:], lambda i, g, qi: (i, 0, 0, 0))]
                + [_const_spec(e) for e in extra])
    scratch = ([pltpu.VMEM((ATTN_STREAMS, tk, tq), F32)] * 2
               + [pltpu.VMEM((ATTN_STREAMS, tk, tq), BF16)] * 2)
    return pl.pallas_call(
        kernel, out_shape=jax.ShapeDtypeStruct((b, D_MODEL, s), BF16),
        grid=(b, n_groups, s // tq), in_specs=in_specs,
        out_specs=pl.BlockSpec((1, D_MODEL // n_groups, tq), lambda i, g, qi: (i, g, qi)),
        scratch_shapes=scratch, compiler_params=_params(3), name=name)(
            q_t, k, v_t, q_norm2, k_norm2, *extra)


def _post_call(o_t, gate_t, x, x_token_major, y_token_major, wout_t, post_gain):
    b, s = _stream_dims(x, x_token_major)
    tok, _ = _tiles(s)
    tile = _stream_spec(tok, False)
    consts = (wout_t, _col(post_gain))
    y_shape = (b, s, D_MODEL) if y_token_major else (b, D_MODEL, s)
    return pl.pallas_call(
        functools.partial(_post_kernel, x_token_major=x_token_major,
                          y_token_major=y_token_major),
        out_shape=jax.ShapeDtypeStruct(y_shape, F32), grid=(b, s // tok),
        in_specs=([tile, tile, _stream_spec(tok, x_token_major)]
                  + [_const_spec(c) for c in consts]),
        out_specs=_stream_spec(tok, y_token_major), compiler_params=_params(2), name="post")(
            o_t, gate_t, x, *consts)


def _rope_tables(pos, dim, theta):
    inv = 1.0 / (theta ** (jnp.arange(0, dim, 2, dtype=F32) / dim))
    ang = pos[:, None] * inv[None, :]
    return jnp.cos(ang).T, jnp.sin(ang).T


def _mla_layer(x, p, x_token_major=False, y_token_major=False):
    _, s = _stream_dims(x, x_token_major)
    w_ukv = p["w_ukv"].reshape(MLA_KV_LORA, MLA_HEADS, MLA_NOPE + MLA_V)
    wv_t = w_ukv[:, :, MLA_NOPE:].reshape(MLA_KV_LORA, MLA_HEADS * MLA_V).T
    wk = jnp.zeros((2 * LANES, MLA_HEADS, DK), F32)
    wk = wk.at[0:MLA_KV_LORA, :, 0:MLA_NOPE].set(w_ukv[:, :, 0:MLA_NOPE])
    eye = jnp.broadcast_to(jnp.eye(MLA_ROPE, dtype=F32)[:, None, :], (MLA_ROPE, MLA_HEADS, MLA_ROPE))
    wk = wk.at[LANES:LANES + MLA_ROPE, :, MLA_NOPE:MLA_QK].set(eye)
    consts = (_col(p["pre_norm"]), p["w_in"].T.astype(BF16), _col(p["q_norm"]),
              p["w_uq"].T.astype(BF16), _col(p["kv_norm"]), wv_t.astype(BF16),
              wk.reshape(2 * LANES, MLA_HEADS * DK).astype(BF16))
    tables = _rope_tables(jnp.arange(s, dtype=F32), MLA_ROPE, MLA_ROPE_THETA)
    pre = _pre_call(_mla_pre_kernel, "mla_pre", x, x_token_major, consts, tables,
                    MLA_HEADS, MLA_HEADS, MLA_HEADS, MLA_V)
    o_t = _attn_call(_mla_attn_kernel, "mla_attn", pre, (), ATTN_STREAMS, lambda g: g,
                     ATTN_STREAMS)
    return _post_call(o_t, pre[3], x, x_token_major, y_token_major, p["w_out"].T.astype(BF16),
                      p["post_norm"])


def _gqa_layer(x, p, x_token_major=False, y_token_major=False):
    _, s = _stream_dims(x, x_token_major)
    consts = (_col(p["pre_norm"]), p["w_in"].T.astype(BF16), _col(p["q_norm"]), _col(p["k_norm"]))
    t_idx = jnp.arange(s)
    half = GQA_HD // 2
    tables = (_rope_tables((t_idx // GRID_W).astype(F32), half, GQA_ROPE_THETA)
              + _rope_tables((t_idx % GRID_W).astype(F32), half, GQA_ROPE_THETA))
    pre = _pre_call(_gqa_pre_kernel, "gqa_pre", x, x_token_major, consts, tables, GQA_HEADS,
                    GQA_KV_HEADS * GQA_HD // DK, GQA_KV_HEADS, GQA_HD)
    assert ATTN_STREAMS == GQA_GROUP
    o_t = _attn_call(_gqa_attn_kernel, "gqa_attn", pre, (), 1, lambda g: g // 2, 1)
    return _post_call(o_t, pre[3], x, x_token_major, y_token_major, p["w_out"].T.astype(BF16),
                      p["post_norm"])


def _diff_layer(x, p, layer_idx, x_token_major=False, y_token_major=False):
    lambda_init = 0.8 - 0.6 * math.exp(-0.3 * layer_idx)
    width = DIFF_HEADS * 2 * DIFF_HD
    w_in = p["w_in"]
    w_qvg = jnp.concatenate([w_in[:, 0:width], w_in[:, 2 * width:4 * width]], axis=1)
    w_k = w_in[:, width:2 * width].reshape(D_MODEL, 2 * DIFF_HEADS, DIFF_HD)
    w_k = jnp.pad(w_k, ((0, 0), (0, 0), (0, DK - DIFF_HD))).reshape(D_MODEL, 2 * DIFF_HEADS * DK)
    consts = (_col(p["pre_norm"]), w_qvg.T.astype(BF16), w_k.astype(BF16))
    pre = _pre_call(_diff_pre_kernel, "diff_pre", x, x_token_major, consts, (),
                    2 * DIFF_HEADS, 2 * DIFF_HEADS, DIFF_HEADS, 2 * DIFF_HD)
    row = lambda v: v.astype(F32).reshape(1, -1)
    extra = (row(p["lambda_q1"]), row(p["lambda_k1"]), row(p["lambda_q2"]), row(p["lambda_k2"]),
             _col(p["subln"]))
    o_t = _attn_call(functools.partial(_diff_attn_kernel, lambda_init=lambda_init), "diff_attn",
                     pre, extra, ATTN_STREAMS, lambda g: g, ATTN_STREAMS // 2)
    return _post_call(o_t, pre[3], x, x_token_major, y_token_major, p["w_out"].T.astype(BF16),
                      p["post_norm"])


def _trunk(x, layers):
    last = len(layers) - 1
    for i, p in enumerate(layers):
        kind = i % 3
        layout = dict(x_token_major=(i == 0), y_token_major=(i == last))
        if kind == 0:
            x = _mla_layer(x, p, **layout)
        elif kind == 1:
            x = _gqa_layer(x, p, **layout)
        else:
            x = _diff_layer(x, p, i, **layout)
    return x


def kernel(x_prompt, x_sample, l0_pre_norm, l0_w_in, l0_q_norm, l0_w_uq, l0_kv_norm, l0_w_ukv, l0_w_out, l0_post_norm, l1_pre_norm, l1_w_in, l1_q_norm, l1_k_norm, l1_w_out, l1_post_norm, l2_pre_norm, l2_w_in, l2_lambda_q1, l2_lambda_k1, l2_lambda_q2, l2_lambda_k2, l2_subln, l2_w_out, l2_post_norm, l3_pre_norm, l3_w_in, l3_q_norm, l3_w_uq, l3_kv_norm, l3_w_ukv, l3_w_out, l3_post_norm):
    layers = [
        dict(pre_norm=l0_pre_norm, w_in=l0_w_in, q_norm=l0_q_norm, w_uq=l0_w_uq,
             kv_norm=l0_kv_norm, w_ukv=l0_w_ukv, w_out=l0_w_out, post_norm=l0_post_norm),
        dict(pre_norm=l1_pre_norm, w_in=l1_w_in, q_norm=l1_q_norm, k_norm=l1_k_norm,
             w_out=l1_w_out, post_norm=l1_post_norm),
        dict(pre_norm=l2_pre_norm, w_in=l2_w_in, lambda_q1=l2_lambda_q1, lambda_k1=l2_lambda_k1,
             lambda_q2=l2_lambda_q2, lambda_k2=l2_lambda_k2, subln=l2_subln,
             w_out=l2_w_out, post_norm=l2_post_norm),
        dict(pre_norm=l3_pre_norm, w_in=l3_w_in, q_norm=l3_q_norm, w_uq=l3_w_uq,
             kv_norm=l3_kv_norm, w_ukv=l3_w_ukv, w_out=l3_w_out, post_norm=l3_post_norm),
    ]
    return (_trunk(x_prompt, layers), _trunk(x_sample, layers))
```

```python
import functools
import math

import jax
import jax.numpy as jnp
from jax import lax
from jax.experimental import pallas as pl
from jax.experimental.pallas import tpu as pltpu

F32 = jnp.float32
BF16 = jnp.bfloat16

D_MODEL = 1024
NORM_EPS = 1e-6
GRID_W = 64

MLA_HEADS = 16
MLA_Q_LORA = 256
MLA_KV_LORA = 128
MLA_NOPE = 64
MLA_ROPE = 32
MLA_V = 64
MLA_QK = MLA_NOPE + MLA_ROPE
MLA_ROPE_THETA = 10000.0

GQA_HEADS = 16
GQA_KV_HEADS = 4
GQA_GROUP = GQA_HEADS // GQA_KV_HEADS
GQA_HD = 64
GQA_ROPE_THETA = 10000.0

DIFF_HEADS = 8
DIFF_HD = 64

LOG2_E = math.log2(math.e)

LANES = 128
BF16_ROWS = 16
DK = LANES
ONES_ROWS = BF16_ROWS
TOK_TILE = 512
Q_TILE = 512
ATTN_STREAMS = 4
SCORE_LIMIT = 60.0
SUBLANES = 8
MXU_ROWS = 256
MXU_COLS = 256
VMEM_LIMIT_BYTES = 56 * 1024 * 1024


def _rms_rows(x, gain_col):
    ms = jnp.mean(x * x, axis=0, keepdims=True)
    return x * lax.rsqrt(ms + NORM_EPS) * gain_col


def _rope_rows(x1, x2, cos, sin):
    return x1 * cos - x2 * sin, x1 * sin + x2 * cos


def _dot(a, b):
    return jnp.dot(a, b, preferred_element_type=F32)


def _feature_major(x_ref, token_major):
    return x_ref[0].T if token_major else x_ref[0]


def _store_norm_tile(worst, norm_out):
    top = jnp.max(jnp.max(worst, axis=0, keepdims=True), axis=1, keepdims=True)
    norm_out[0, 0] = jnp.broadcast_to(top, norm_out.shape[2:])


def _store_q_block(q_f32, q_out, head, worst):
    qb = q_f32.astype(BF16)
    q_out[0, head] = qb
    qf = qb.astype(F32)
    n2 = jnp.sum(qf * qf, axis=0, keepdims=True)
    return n2 if worst is None else jnp.maximum(worst, n2)


def _store_k_blocks(k_nat, k_out, knorm_out):
    worst = None
    for j in range(k_out.shape[1]):
        kb = k_nat[:, j * DK:(j + 1) * DK].astype(BF16)
        k_out[0, j] = kb
        kf = kb.astype(F32)
        n2 = jnp.sum(kf * kf, axis=1, keepdims=True)
        worst = n2 if worst is None else jnp.maximum(worst, n2)
    _store_norm_tile(worst, knorm_out)


def _mla_pre_kernel(x_ref, g_ref, win_ref, qn_ref, wuq_ref, kvn_ref, wv_ref, wk_ref,
                    cos_ref, sin_ref, q_out, k_out, v_out, gate_out, qnorm_out, knorm_out,
                    *, x_token_major):
    x = _feature_major(x_ref, x_token_major)
    t = x.shape[1]
    h = _rms_rows(x, g_ref[...]).astype(BF16)
    cos = cos_ref[...]
    sin = sin_ref[...]
    scale = MLA_QK ** -0.5 * LOG2_E

    cq = _dot(win_ref[0:MLA_Q_LORA, :], h)
    cqn = _rms_rows(cq, qn_ref[...]).astype(BF16)
    q_all = _dot(wuq_ref[...], cqn) * scale
    zeros_q = jnp.zeros((DK - MLA_QK, t), F32)
    half = MLA_ROPE // 2
    worst = None
    for hd in range(MLA_HEADS):
        base = hd * MLA_QK
        nope = q_all[base:base + MLA_NOPE]
        r1, r2 = _rope_rows(q_all[base + MLA_NOPE:base + MLA_NOPE + half],
                            q_all[base + MLA_NOPE + half:base + MLA_QK], cos, sin)
        worst = _store_q_block(jnp.concatenate([nope, r1, r2, zeros_q], axis=0), q_out, hd, worst)
    _store_norm_tile(worst, qnorm_out)

    kv_lo = MLA_Q_LORA
    kv_hi = MLA_Q_LORA + MLA_KV_LORA + MLA_ROPE
    ckr = _dot(win_ref[kv_lo:kv_hi, :], h)
    ckvn = _rms_rows(ckr[0:MLA_KV_LORA], kvn_ref[...])
    v_all = _dot(wv_ref[...], ckvn.astype(BF16))
    ones = jnp.ones((ONES_ROWS, t), BF16)
    for hd in range(MLA_HEADS):
        v_out[0, hd, 0, 0:MLA_V, :] = v_all[hd * MLA_V:(hd + 1) * MLA_V].astype(BF16)
        v_out[0, hd, 0, MLA_V:MLA_V + ONES_ROWS, :] = ones

    kr = ckr[MLA_KV_LORA:MLA_KV_LORA + MLA_ROPE]
    r1, r2 = _rope_rows(kr[0:half], kr[half:MLA_ROPE], cos, sin)
    krr = jnp.concatenate([r1, r2, jnp.zeros((LANES - MLA_ROPE, t), F32)], axis=0)
    z = jnp.concatenate([ckvn.T, krr.T], axis=1).astype(BF16)
    k_all = _dot(z, wk_ref[...])
    _store_k_blocks(k_all, k_out, knorm_out)

    gate_out[0] = _dot(win_ref[kv_hi:kv_hi + D_MODEL, :], h).astype(BF16)


def _gqa_pre_kernel(x_ref, g_ref, win_ref, qn_ref, kn_ref, cr_ref, sr_ref, cc_ref, sc_ref,
                    q_out, k_out, v_out, gate_out, qnorm_out, knorm_out, *, x_token_major):
    x = _feature_major(x_ref, x_token_major)
    t = x.shape[1]
    h = _rms_rows(x, g_ref[...]).astype(BF16)
    cr, sr, cc, sc = cr_ref[...], sr_ref[...], cc_ref[...], sc_ref[...]
    scale = GQA_HD ** -0.5 * LOG2_E
    n_q = GQA_HEADS * GQA_HD
    n_kv = GQA_KV_HEADS * GQA_HD
    qk = _dot(win_ref[0:n_q + n_kv, :], h)
    zeros_q = jnp.zeros((DK - GQA_HD, t), F32)
    quarter = GQA_HD // 4

    def norm_rope(blk, gain_col):
        xn = _rms_rows(blk, gain_col)
        a, b = _rope_rows(xn[0:quarter], xn[quarter:2 * quarter], cr, sr)
        c, d = _rope_rows(xn[2 * quarter:3 * quarter], xn[3 * quarter:4 * quarter], cc, sc)
        return jnp.concatenate([a, b, c, d], axis=0)

    worst = None
    for hd in range(GQA_HEADS):
        rq = norm_rope(qk[hd * GQA_HD:(hd + 1) * GQA_HD], qn_ref[...]) * scale
        kv = hd // GQA_GROUP
        parts = [rq, zeros_q] if kv % 2 == 0 else [zeros_q, rq]
        worst = _store_q_block(jnp.concatenate(parts, axis=0), q_out, hd, worst)
    _store_norm_tile(worst, qnorm_out)

    k_rows = [norm_rope(qk[n_q + kv * GQA_HD:n_q + (kv + 1) * GQA_HD], kn_ref[...])
              for kv in range(GQA_KV_HEADS)]
    k_nat = jnp.concatenate(k_rows, axis=0).T
    _store_k_blocks(k_nat, k_out, knorm_out)

    v = _dot(win_ref[n_q + n_kv:n_q + 2 * n_kv, :], h)
    ones = jnp.ones((ONES_ROWS, t), BF16)
    for kv in range(GQA_KV_HEADS):
        v_out[0, kv, 0, 0:GQA_HD, :] = v[kv * GQA_HD:(kv + 1) * GQA_HD].astype(BF16)
        v_out[0, kv, 0, GQA_HD:GQA_HD + ONES_ROWS, :] = ones

    g_lo = n_q + 2 * n_kv
    gate_out[0] = _dot(win_ref[g_lo:g_lo + D_MODEL, :], h).astype(BF16)


def _diff_pre_kernel(x_ref, g_ref, win_ref, wk_ref, q_out, k_out, v_out, gate_out, qnorm_out,
                     knorm_out, *, x_token_major):
    x = _feature_major(x_ref, x_token_major)
    t = x.shape[1]
    h = _rms_rows(x, g_ref[...]).astype(BF16)
    scale = DIFF_HD ** -0.5 * LOG2_E
    width = DIFF_HEADS * 2 * DIFF_HD
    q = _dot(win_ref[0:width, :], h) * scale
    zeros_q = jnp.zeros((DK - DIFF_HD, t), F32)
    worst = None
    for hc in range(2 * DIFF_HEADS):
        blk = q[hc * DIFF_HD:(hc + 1) * DIFF_HD]
        parts = [blk, zeros_q] if hc % 2 == 0 else [zeros_q, blk]
        worst = _store_q_block(jnp.concatenate(parts, axis=0), q_out, hc, worst)
    _store_norm_tile(worst, qnorm_out)

    k_nat = lax.dot_general(h, wk_ref[...], (((0,), (0,)), ((), ())),
                            preferred_element_type=F32)
    _store_k_blocks(k_nat, k_out, knorm_out)

    v = _dot(win_ref[width:2 * width, :], h)
    dv = 2 * DIFF_HD
    ones = jnp.ones((ONES_ROWS, t), BF16)
    for hd in range(DIFF_HEADS):
        v_out[0, hd, 0, 0:dv, :] = v[hd * dv:(hd + 1) * dv].astype(BF16)
        v_out[0, hd, 0, dv:dv + ONES_ROWS, :] = ones

    gate_out[0] = _dot(win_ref[2 * width:3 * width, :], h).astype(BF16)


def _attend(q_ts, k_chunk, v_chunk, nkc, dv, scratch, bias_fn=None):
    n = len(q_ts)
    tq = q_ts[0].shape[1]
    s_slots, p_slots = scratch[0:2], scratch[2:4]
    dvx = dv + ONES_ROWS
    assert nkc >= 2 and nkc % 2 == 0

    def qk(c, s_ref):
        maxima = []
        for g in range(n):
            s = _dot(k_chunk(g, c), q_ts[g])
            if bias_fn is not None:
                s = s + bias_fn(g, c)
            s_ref[g] = s
            maxima.append(jnp.max(s, axis=0, keepdims=True))
        return tuple(maxima)

    def softmax(s_ref, s_max, p_ref, m):
        m_out, alpha = [], []
        for g in range(n):
            m_new = jnp.maximum(m[g], s_max[g])
            alpha.append(jnp.exp2(m[g] - m_new))
            p_ref[g] = jnp.exp2(s_ref[g] - m_new).astype(BF16)
            m_out.append(m_new)
        return tuple(m_out), tuple(alpha)

    def pv(c, p_ref, alpha, acc):
        return tuple(alpha[g] * acc[g] + _dot(v_chunk(g, c), p_ref[g]) for g in range(n))

    m = tuple(jnp.full((1, tq), -jnp.inf, F32) for _ in range(n))
    acc = tuple(jnp.zeros((dvx, tq), F32) for _ in range(n))
    max0 = qk(0, s_slots[0])
    max1 = qk(1, s_slots[1])
    m, alpha = softmax(s_slots[0], max0, p_slots[0], m)

    def pair(j, carry):
        m, alpha, acc, max1 = carry
        t = 2 * j + 1
        max0 = qk(t + 1, s_slots[0])
        acc = pv(t - 1, p_slots[0], alpha, acc)
        m, alpha = softmax(s_slots[1], max1, p_slots[1], m)
        max1 = qk(t + 2, s_slots[1])
        acc = pv(t, p_slots[1], alpha, acc)
        m, alpha = softmax(s_slots[0], max0, p_slots[0], m)
        return m, alpha, acc, max1

    m, alpha, acc, max1 = lax.fori_loop(0, (nkc - 2) // 2, pair, (m, alpha, acc, max1))
    acc = pv(nkc - 2, p_slots[0], alpha, acc)
    m, alpha = softmax(s_slots[1], max1, p_slots[1], m)
    acc = pv(nkc - 1, p_slots[1], alpha, acc)
    return [a[0:dv] * (1.0 / a[dv:dv + 1]) for a in acc]


def _attend_bounded(q_ts, k_chunk, v_chunk, nkc, dv, scratch, bias_fn=None):
    n = len(q_ts)
    p_slots = scratch[2:4]
    tk, tq = p_slots[0].shape[1:]
    valu_den = bias_fn is None
    v_rows = dv if valu_den else dv + ONES_ROWS
    col_tiles = [slice(lo, lo + MXU_COLS) for lo in range(0, tq, MXU_COLS)]
    row_tiles = [slice(lo, lo + MXU_ROWS) for lo in range(0, tk, MXU_ROWS)]

    def stage(t, acc, den):
        p_new = p_slots[t % 2] if t < nkc else None
        p_old = p_slots[(t - 1) % 2] if t > 0 else None
        ks = [None if p_new is None else k_chunk(g, t) for g in range(n)]
        vs = [None if p_old is None else v_chunk(g, t - 1)[0:v_rows] for g in range(n)]
        biases = [None if (bias_fn is None or p_new is None) else bias_fn(g, t)
                  for g in range(n)]
        parts = [[acc[g][:, cols] for cols in col_tiles] for g in range(n)]
        dens = [[den[g][:, cols] for cols in col_tiles] for g in range(n)]
        for rows in row_tiles:
            for ci, cols in enumerate(col_tiles):
                for g in range(n):
                    if p_new is not None:
                        s = _dot(ks[g][rows], q_ts[g][:, cols])
                        if biases[g] is not None:
                            s = s + biases[g][rows, cols]
                        p = jnp.exp2(s)
                        p_new[g, rows, cols] = p.astype(BF16)
                        if valu_den:
                            dens[g][ci] = dens[g][ci] + jnp.sum(
                                p.reshape(-1, SUBLANES, MXU_COLS), axis=0)
                    if p_old is not None:
                        parts[g][ci] = parts[g][ci] + _dot(vs[g][:, rows], p_old[g, rows, cols])
        return (tuple(jnp.concatenate(parts[g], axis=1) for g in range(n)),
                tuple(jnp.concatenate(dens[g], axis=1) for g in range(n)))

    acc = tuple(jnp.zeros((v_rows, tq), F32) for _ in range(n))
    den = tuple(jnp.zeros((SUBLANES, tq), F32) for _ in range(n))
    for t in range(nkc + 1):
        acc, den = stage(t, acc, den)
    if valu_den:
        return [a * (1.0 / jnp.sum(d, axis=0, keepdims=True)) for a, d in zip(acc, den)]
    return [a[0:dv] * (1.0 / a[dv:dv + 1]) for a in acc]


def _key_rows(c, tk):
    if isinstance(c, int):
        return pl.ds(c * tk, tk)
    return pl.ds(pl.multiple_of(c * tk, tk), tk)


def _attend_dispatch(q_ts, k_chunk, v_chunk, norm_refs, nkc, dv, scratch, finish, bias_fn=None):
    qn_ref, kn_ref = norm_refs
    q_max2 = jnp.max(qn_ref[0, pl.program_id(2)])
    bounded = q_max2 * jnp.max(kn_ref[0]) <= SCORE_LIMIT * SCORE_LIMIT

    @pl.when(bounded)
    def _():
        finish(_attend_bounded(q_ts, k_chunk, v_chunk, nkc, dv, scratch, bias_fn))

    @pl.when(jnp.logical_not(bounded))
    def _():
        finish(_attend(q_ts, k_chunk, v_chunk, nkc, dv, scratch, bias_fn))


def _mla_attn_kernel(q_ref, k_ref, v_ref, qn2_ref, kn2_ref, o_ref, *scratch):
    nkc, tk = v_ref.shape[2], v_ref.shape[4]

    def finish(outs):
        for g, o in enumerate(outs):
            o_ref[0, g * MLA_V:(g + 1) * MLA_V, :] = o.astype(BF16)

    _attend_dispatch([q_ref[0, g] for g in range(ATTN_STREAMS)],
                     lambda g, c: k_ref[0, g, _key_rows(c, tk), :],
                     lambda g, c: v_ref[0, g, c], (qn2_ref, kn2_ref), nkc, MLA_V, scratch,
                     finish)


def _gqa_attn_kernel(q_ref, k_ref, v_ref, qn2_ref, kn2_ref, o_ref, *scratch):
    nkc, tk = v_ref.shape[2], v_ref.shape[4]

    def finish(outs):
        for g, o in enumerate(outs):
            o_ref[0, g * GQA_HD:(g + 1) * GQA_HD, :] = o.astype(BF16)

    _attend_dispatch([q_ref[0, g] for g in range(ATTN_STREAMS)],
                     lambda g, c: k_ref[0, 0, _key_rows(c, tk), :],
                     lambda g, c: v_ref[0, 0, c], (qn2_ref, kn2_ref), nkc, GQA_HD, scratch,
                     finish)


def _diff_attn_kernel(q_ref, k_ref, v_ref, qn2_ref, kn2_ref, lq1_ref, lk1_ref, lq2_ref, lk2_ref,
                      subln_ref, o_ref, *scratch, lambda_init):
    tq = q_ref.shape[3]
    nkc, tk = v_ref.shape[2], v_ref.shape[4]
    heads = ATTN_STREAMS // 2
    dv = 2 * DIFF_HD
    lam = (jnp.exp(jnp.sum(lq1_ref[...] * lk1_ref[...], axis=1, keepdims=True))
           - jnp.exp(jnp.sum(lq2_ref[...] * lk2_ref[...], axis=1, keepdims=True))
           + lambda_init)
    q_pos = (pl.program_id(2) * tq + lax.broadcasted_iota(jnp.int32, (1, tq), 1)).astype(F32)
    k_iota = lax.broadcasted_iota(jnp.int32, (tk, 1), 0)
    slopes = []
    for hd in range(heads):
        exponent = jnp.full((1, 1), 126 - (pl.program_id(1) * heads + hd), jnp.int32)
        slopes.append(lax.bitcast_convert_type(exponent << 23, F32) * LOG2_E)

    def bias_fn(g, c):
        k_pos = (c * tk + k_iota).astype(F32)
        return -slopes[g // 2] * jnp.abs(q_pos - k_pos)

    def finish(outs):
        for hd in range(heads):
            o = outs[2 * hd] - lam * outs[2 * hd + 1]
            o = _rms_rows(o, subln_ref[...]) * (1.0 - lambda_init)
            o_ref[0, hd * dv:(hd + 1) * dv, :] = o.astype(BF16)

    _attend_dispatch([q_ref[0, g] for g in range(ATTN_STREAMS)],
                     lambda g, c: k_ref[0, g // 2, _key_rows(c, tk), :],
                     lambda g, c: v_ref[0, g // 2, c], (qn2_ref, kn2_ref), nkc, dv, scratch, finish,
                     bias_fn)


def _post_kernel(o_ref, gate_ref, x_ref, wout_ref, g_ref, y_ref, *, x_token_major,
                 y_token_major):
    gate = gate_ref[0].astype(F32)
    og = (o_ref[0].astype(F32) * (gate / (1.0 + jnp.exp(-gate)))).astype(BF16)
    m = _dot(wout_ref[...], og)
    y = _feature_major(x_ref, x_token_major) + _rms_rows(m, g_ref[...])
    y_ref[0] = y.T if y_token_major else y


def _params(n_axes):
    return pltpu.CompilerParams(dimension_semantics=("arbitrary",) * n_axes,
                                vmem_limit_bytes=VMEM_LIMIT_BYTES)


def _const_spec(arr):
    nd = arr.ndim
    return pl.BlockSpec(arr.shape, lambda *_: (0,) * nd)


def _col(v):
    return v.astype(F32).reshape(-1, 1)


def _tiles(s):
    tok = min(TOK_TILE, s)
    tq = min(Q_TILE, s)
    assert s % tok == 0 and s % tq == 0
    return tok, tq


def _stream_dims(x, token_major):
    return (x.shape[0], x.shape[1]) if token_major else (x.shape[0], x.shape[2])


def _stream_spec(tok, token_major):
    if token_major:
        return pl.BlockSpec((1, tok, D_MODEL), lambda i, t: (i, t, 0))
    return pl.BlockSpec((1, D_MODEL, tok), lambda i, t: (i, 0, t))


def _pre_call(kernel, name, x, x_token_major, consts, tables, n_heads, n_kblocks, n_vheads, dv):
    b, s = _stream_dims(x, x_token_major)
    tok, _ = _tiles(s)
    dvx = dv + ONES_ROWS
    in_specs = ([_stream_spec(tok, x_token_major)]
                + [_const_spec(c) for c in consts]
                + [pl.BlockSpec((tb.shape[0], tok), lambda i, t: (0, t)) for tb in tables])
    norm_shape = jax.ShapeDtypeStruct((b, s // tok, SUBLANES, LANES), F32)
    norm_spec = pl.BlockSpec((1, 1, SUBLANES, LANES), lambda i, t: (i, t, 0, 0))
    out_shape = (jax.ShapeDtypeStruct((b, n_heads, DK, s), BF16),
                 jax.ShapeDtypeStruct((b, n_kblocks, s, DK), BF16),
                 jax.ShapeDtypeStruct((b, n_vheads, s // tok, dvx, tok), BF16),
                 jax.ShapeDtypeStruct((b, D_MODEL, s), BF16),
                 norm_shape, norm_shape)
    out_specs = (pl.BlockSpec((1, n_heads, DK, tok), lambda i, t: (i, 0, 0, t)),
                 pl.BlockSpec((1, n_kblocks, tok, DK), lambda i, t: (i, 0, t, 0)),
                 pl.BlockSpec((1, n_vheads, 1, dvx, tok), lambda i, t: (i, 0, t, 0, 0)),
                 pl.BlockSpec((1, D_MODEL, tok), lambda i, t: (i, 0, t)),
                 norm_spec, norm_spec)
    return pl.pallas_call(functools.partial(kernel, x_token_major=x_token_major),
                          out_shape=out_shape, grid=(b, s // tok), in_specs=in_specs,
                          out_specs=out_specs, compiler_params=_params(2), name=name)(
                              x, *consts, *tables)


def _attn_call(kernel, name, pre_outs, extra, k_blocks, k_block_of_group, v_heads):
    q_t, k, v_t, _, q_norm2, k_norm2 = pre_outs
    b, n_heads, _, s = q_t.shape
    tok, tq = _tiles(s)
    assert tq == tok
    nkc, dvx, tk = v_t.shape[2:]
    n_groups = n_heads // ATTN_STREAMS
    assert n_groups * v_heads == v_t.shape[1]
    in_specs = ([pl.BlockSpec((1, ATTN_STREAMS, DK, tq), lambda i, g, qi: (i, g, 0, qi)),
                 pl.BlockSpec((1, k_blocks, s, DK),
                              lambda i, g, qi: (i, k_block_of_group(g), 0, 0)),
                 pl.BlockSpec((1, v_heads, nkc, dvx, tk), lambda i, g, qi: (i, g, 0, 0, 0)),
                 pl.BlockSpec((1,) + q_norm2.shape[1:], lambda i, g, qi: (i, 0, 0, 0)),
                 pl.BlockSpec((1,) + k_norm2.shape[1:], lambda i, g, qi: (i, 0, 0, 0))]
                + [_const_spec(e) for e in extra])
    scratch = ([pltpu.VMEM((ATTN_STREAMS, tk, tq), F32)] * 2
               + [pltpu.VMEM((ATTN_STREAMS, tk, tq), BF16)] * 2)
    return pl.pallas_call(
        kernel, out_shape=jax.ShapeDtypeStruct((b, D_MODEL, s), BF16),
        grid=(b, n_groups, s // tq), in_specs=in_specs,
        out_specs=pl.BlockSpec((1, D_MODEL // n_groups, tq), lambda i, g, qi: (i, g, qi)),
        scratch_shapes=scratch, compiler_params=_params(3), name=name)(
            q_t, k, v_t, q_norm2, k_norm2, *extra)


def _post_call(o_t, gate_t, x, x_token_major, y_token_major, wout_t, post_gain):
    b, s = _stream_dims(x, x_token_major)
    tok, _ = _tiles(s)
    tile = _stream_spec(tok, False)
    consts = (wout_t, _col(post_gain))
    y_shape = (b, s, D_MODEL) if y_token_major else (b, D_MODEL, s)
    return pl.pallas_call(
        functools.partial(_post_kernel, x_token_major=x_token_major,
                          y_token_major=y_token_major),
        out_shape=jax.ShapeDtypeStruct(y_shape, F32), grid=(b, s // tok),
        in_specs=([tile, tile, _stream_spec(tok, x_token_major)]
                  + [_const_spec(c) for c in consts]),
        out_specs=_stream_spec(tok, y_token_major), compiler_params=_params(2), name="post")(
            o_t, gate_t, x, *consts)


def _rope_tables(pos, dim, theta):
    inv = 1.0 / (theta ** (jnp.arange(0, dim, 2, dtype=F32) / dim))
    ang = pos[:, None] * inv[None, :]
    return jnp.cos(ang).T, jnp.sin(ang).T


def _mla_layer(x, p, x_token_major=False, y_token_major=False):
    _, s = _stream_dims(x, x_token_major)
    w_ukv = p["w_ukv"].reshape(MLA_KV_LORA, MLA_HEADS, MLA_NOPE + MLA_V)
    wv_t = w_ukv[:, :, MLA_NOPE:].reshape(MLA_KV_LORA, MLA_HEADS * MLA_V).T
    wk = jnp.zeros((2 * LANES, MLA_HEADS, DK), F32)
    wk = wk.at[0:MLA_KV_LORA, :, 0:MLA_NOPE].set(w_ukv[:, :, 0:MLA_NOPE])
    eye = jnp.broadcast_to(jnp.eye(MLA_ROPE, dtype=F32)[:, None, :], (MLA_ROPE, MLA_HEADS, MLA_ROPE))
    wk = wk.at[LANES:LANES + MLA_ROPE, :, MLA_NOPE:MLA_QK].set(eye)
    consts = (_col(p["pre_norm"]), p["w_in"].T.astype(BF16), _col(p["q_norm"]),
              p["w_uq"].T.astype(BF16), _col(p["kv_norm"]), wv_t.astype(BF16),
              wk.reshape(2 * LANES, MLA_HEADS * DK).astype(BF16))
    tables = _rope_tables(jnp.arange(s, dtype=F32), MLA_ROPE, MLA_ROPE_THETA)
    pre = _pre_call(_mla_pre_kernel, "mla_pre", x, x_token_major, consts, tables,
                    MLA_HEADS, MLA_HEADS, MLA_HEADS, MLA_V)
    o_t = _attn_call(_mla_attn_kernel, "mla_attn", pre, (), ATTN_STREAMS, lambda g: g,
                     ATTN_STREAMS)
    return _post_call(o_t, pre[3], x, x_token_major, y_token_major, p["w_out"].T.astype(BF16),
                      p["post_norm"])


def _gqa_layer(x, p, x_token_major=False, y_token_major=False):
    _, s = _stream_dims(x, x_token_major)
    consts = (_col(p["pre_norm"]), p["w_in"].T.astype(BF16), _col(p["q_norm"]), _col(p["k_norm"]))
    t_idx = jnp.arange(s)
    half = GQA_HD // 2
    tables = (_rope_tables((t_idx // GRID_W).astype(F32), half, GQA_ROPE_THETA)
              + _rope_tables((t_idx % GRID_W).astype(F32), half, GQA_ROPE_THETA))
    pre = _pre_call(_gqa_pre_kernel, "gqa_pre", x, x_token_major, consts, tables, GQA_HEADS,
                    GQA_KV_HEADS * GQA_HD // DK, GQA_KV_HEADS, GQA_HD)
    assert ATTN_STREAMS == GQA_GROUP
    o_t = _attn_call(_gqa_attn_kernel, "gqa_attn", pre, (), 1, lambda g: g // 2, 1)
    return _post_call(o_t, pre[3], x, x_token_major, y_token_major, p["w_out"].T.astype(BF16),
                      p["post_norm"])


def _diff_layer(x, p, layer_idx, x_token_major=False, y_token_major=False):
    lambda_init = 0.8 - 0.6 * math.exp(-0.3 * layer_idx)
    width = DIFF_HEADS * 2 * DIFF_HD
    w_in = p["w_in"]
    w_qvg = jnp.concatenate([w_in[:, 0:width], w_in[:, 2 * width:4 * width]], axis=1)
    consts = (_col(p["pre_norm"]), w_qvg.T.astype(BF16), w_in[:, width:2 * width].astype(BF16))
    pre = _pre_call(_diff_pre_kernel, "diff_pre", x, x_token_major, consts, (),
                    2 * DIFF_HEADS, DIFF_HEADS, DIFF_HEADS, 2 * DIFF_HD)
    row = lambda v: v.astype(F32).reshape(1, -1)
    extra = (row(p["lambda_q1"]), row(p["lambda_k1"]), row(p["lambda_q2"]), row(p["lambda_k2"]),
             _col(p["subln"]))
    o_t = _attn_call(functools.partial(_diff_attn_kernel, lambda_init=lambda_init), "diff_attn",
                     pre, extra, ATTN_STREAMS // 2, lambda g: g, ATTN_STREAMS // 2)
    return _post_call(o_t, pre[3], x, x_token_major, y_token_major, p["w_out"].T.astype(BF16),
                      p["post_norm"])


def _trunk(x, layers):
    last = len(layers) - 1
    for i, p in enumerate(layers):
        kind = i % 3
        layout = dict(x_token_major=(i == 0), y_token_major=(i == last))
        if kind == 0:
            x = _mla_layer(x, p, **layout)
        elif kind == 1:
            x = _gqa_layer(x, p, **layout)
        else:
            x = _diff_layer(x, p, i, **layout)
    return x


def kernel(x_prompt, x_sample, l0_pre_norm, l0_w_in, l0_q_norm, l0_w_uq, l0_kv_norm, l0_w_ukv, l0_w_out, l0_post_norm, l1_pre_norm, l1_w_in, l1_q_norm, l1_k_norm, l1_w_out, l1_post_norm, l2_pre_norm, l2_w_in, l2_lambda_q1, l2_lambda_k1, l2_lambda_q2, l2_lambda_k2, l2_subln, l2_w_out, l2_post_norm, l3_pre_norm, l3_w_in, l3_q_norm, l3_w_uq, l3_kv_norm, l3_w_ukv, l3_w_out, l3_post_norm):
    layers = [
        dict(pre_norm=l0_pre_norm, w_in=l0_w_in, q_norm=l0_q_norm, w_uq=l0_w_uq,
             kv_norm=l0_kv_norm, w_ukv=l0_w_ukv, w_out=l0_w_out, post_norm=l0_post_norm),
        dict(pre_norm=l1_pre_norm, w_in=l1_w_in, q_norm=l1_q_norm, k_norm=l1_k_norm,
             w_out=l1_w_out, post_norm=l1_post_norm),
        dict(pre_norm=l2_pre_norm, w_in=l2_w_in, lambda_q1=l2_lambda_q1, lambda_k1=l2_lambda_k1,
             lambda_q2=l2_lambda_q2, lambda_k2=l2_lambda_k2, subln=l2_subln,
             w_out=l2_w_out, post_norm=l2_post_norm),
        dict(pre_norm=l3_pre_norm, w_in=l3_w_in, q_norm=l3_q_norm, w_uq=l3_w_uq,
             kv_norm=l3_kv_norm, w_ukv=l3_w_ukv, w_out=l3_w_out, post_norm=l3_post_norm),
    ]
    return (_trunk(x_prompt, layers), _trunk(x_sample, layers))
```

```python
import functools
import math

import jax
import jax.numpy as jnp
from jax import lax
from jax.experimental import pallas as pl
from jax.experimental.pallas import tpu as pltpu

F32 = jnp.float32
BF16 = jnp.bfloat16

D_MODEL = 1024
NORM_EPS = 1e-6
GRID_W = 64

MLA_HEADS = 16
MLA_Q_LORA = 256
MLA_KV_LORA = 128
MLA_NOPE = 64
MLA_ROPE = 32
MLA_V = 64
MLA_QK = MLA_NOPE + MLA_ROPE
MLA_ROPE_THETA = 10000.0

GQA_HEADS = 16
GQA_KV_HEADS = 4
GQA_GROUP = GQA_HEADS // GQA_KV_HEADS
GQA_HD = 64
GQA_ROPE_THETA = 10000.0

DIFF_HEADS = 8
DIFF_HD = 64

LOG2_E = math.log2(math.e)

LANES = 128
BF16_ROWS = 16
DK = LANES
ONES_ROWS = BF16_ROWS
TOK_TILE = 512
Q_TILE = 512
ATTN_STREAMS = 4
SCORE_LIMIT = 60.0
SUBLANES = 8
MXU_ROWS = 256
MXU_COLS = 256
VMEM_LIMIT_BYTES = 56 * 1024 * 1024


def _rms_rows(x, gain_col):
    ms = jnp.mean(x * x, axis=0, keepdims=True)
    return x * lax.rsqrt(ms + NORM_EPS) * gain_col


def _rope_rows(x1, x2, cos, sin):
    return x1 * cos - x2 * sin, x1 * sin + x2 * cos


def _dot(a, b):
    return jnp.dot(a, b, preferred_element_type=F32)


def _feature_major(x_ref, token_major):
    return x_ref[0].T if token_major else x_ref[0]


def _store_norm_tile(worst, norm_out):
    top = jnp.max(jnp.max(worst, axis=0, keepdims=True), axis=1, keepdims=True)
    norm_out[0, 0] = jnp.broadcast_to(top, norm_out.shape[2:])


def _store_q_block(q_f32, q_out, head, worst):
    qb = q_f32.astype(BF16)
    q_out[0, head] = qb
    qf = qb.astype(F32)
    n2 = jnp.sum(qf * qf, axis=0, keepdims=True)
    return n2 if worst is None else jnp.maximum(worst, n2)


def _store_k_blocks(k_nat, k_out, knorm_out):
    worst = None
    for j in range(k_out.shape[1]):
        kb = k_nat[:, j * DK:(j + 1) * DK].astype(BF16)
        k_out[0, j] = kb
        kf = kb.astype(F32)
        n2 = jnp.sum(kf * kf, axis=1, keepdims=True)
        worst = n2 if worst is None else jnp.maximum(worst, n2)
    _store_norm_tile(worst, knorm_out)


def _mla_pre_kernel(x_ref, g_ref, win_ref, qn_ref, wuq_ref, kvn_ref, wv_ref, wk_ref,
                    cos_ref, sin_ref, q_out, k_out, v_out, gate_out, qnorm_out, knorm_out,
                    *, x_token_major):
    x = _feature_major(x_ref, x_token_major)
    t = x.shape[1]
    h = _rms_rows(x, g_ref[...]).astype(BF16)
    cos = cos_ref[...]
    sin = sin_ref[...]
    scale = MLA_QK ** -0.5 * LOG2_E

    cq = _dot(win_ref[0:MLA_Q_LORA, :], h)
    cqn = _rms_rows(cq, qn_ref[...]).astype(BF16)
    q_all = _dot(wuq_ref[...], cqn) * scale
    zeros_q = jnp.zeros((DK - MLA_QK, t), F32)
    half = MLA_ROPE // 2
    worst = None
    for hd in range(MLA_HEADS):
        base = hd * MLA_QK
        nope = q_all[base:base + MLA_NOPE]
        r1, r2 = _rope_rows(q_all[base + MLA_NOPE:base + MLA_NOPE + half],
                            q_all[base + MLA_NOPE + half:base + MLA_QK], cos, sin)
        worst = _store_q_block(jnp.concatenate([nope, r1, r2, zeros_q], axis=0), q_out, hd, worst)
    _store_norm_tile(worst, qnorm_out)

    kv_lo = MLA_Q_LORA
    kv_hi = MLA_Q_LORA + MLA_KV_LORA + MLA_ROPE
    ckr = _dot(win_ref[kv_lo:kv_hi, :], h)
    ckvn = _rms_rows(ckr[0:MLA_KV_LORA], kvn_ref[...])
    v_all = _dot(wv_ref[...], ckvn.astype(BF16))
    ones = jnp.ones((ONES_ROWS, t), BF16)
    for hd in range(MLA_HEADS):
        v_out[0, hd, 0, 0:MLA_V, :] = v_all[hd * MLA_V:(hd + 1) * MLA_V].astype(BF16)
        v_out[0, hd, 0, MLA_V:MLA_V + ONES_ROWS, :] = ones

    kr = ckr[MLA_KV_LORA:MLA_KV_LORA + MLA_ROPE]
    r1, r2 = _rope_rows(kr[0:half], kr[half:MLA_ROPE], cos, sin)
    krr = jnp.concatenate([r1, r2, jnp.zeros((LANES - MLA_ROPE, t), F32)], axis=0)
    z = jnp.concatenate([ckvn.T, krr.T], axis=1).astype(BF16)
    k_all = _dot(z, wk_ref[...])
    _store_k_blocks(k_all, k_out, knorm_out)

    gate_out[0] = _dot(win_ref[kv_hi:kv_hi + D_MODEL, :], h).astype(BF16)


def _gqa_pre_kernel(x_ref, g_ref, win_ref, qn_ref, kn_ref, cr_ref, sr_ref, cc_ref, sc_ref,
                    q_out, k_out, v_out, gate_out, qnorm_out, knorm_out, *, x_token_major):
    x = _feature_major(x_ref, x_token_major)
    t = x.shape[1]
    h = _rms_rows(x, g_ref[...]).astype(BF16)
    cr, sr, cc, sc = cr_ref[...], sr_ref[...], cc_ref[...], sc_ref[...]
    scale = GQA_HD ** -0.5 * LOG2_E
    n_q = GQA_HEADS * GQA_HD
    n_kv = GQA_KV_HEADS * GQA_HD
    qk = _dot(win_ref[0:n_q + n_kv, :], h)
    zeros_q = jnp.zeros((DK - GQA_HD, t), F32)
    quarter = GQA_HD // 4

    def norm_rope(blk, gain_col):
        xn = _rms_rows(blk, gain_col)
        a, b = _rope_rows(xn[0:quarter], xn[quarter:2 * quarter], cr, sr)
        c, d = _rope_rows(xn[2 * quarter:3 * quarter], xn[3 * quarter:4 * quarter], cc, sc)
        return jnp.concatenate([a, b, c, d], axis=0)

    worst = None
    for hd in range(GQA_HEADS):
        rq = norm_rope(qk[hd * GQA_HD:(hd + 1) * GQA_HD], qn_ref[...]) * scale
        kv = hd // GQA_GROUP
        parts = [rq, zeros_q] if kv % 2 == 0 else [zeros_q, rq]
        worst = _store_q_block(jnp.concatenate(parts, axis=0), q_out, hd, worst)
    _store_norm_tile(worst, qnorm_out)

    k_rows = [norm_rope(qk[n_q + kv * GQA_HD:n_q + (kv + 1) * GQA_HD], kn_ref[...])
              for kv in range(GQA_KV_HEADS)]
    k_nat = jnp.concatenate(k_rows, axis=0).T
    _store_k_blocks(k_nat, k_out, knorm_out)

    v = _dot(win_ref[n_q + n_kv:n_q + 2 * n_kv, :], h)
    ones = jnp.ones((ONES_ROWS, t), BF16)
    for kv in range(GQA_KV_HEADS):
        v_out[0, kv, 0, 0:GQA_HD, :] = v[kv * GQA_HD:(kv + 1) * GQA_HD].astype(BF16)
        v_out[0, kv, 0, GQA_HD:GQA_HD + ONES_ROWS, :] = ones

    g_lo = n_q + 2 * n_kv
    gate_out[0] = _dot(win_ref[g_lo:g_lo + D_MODEL, :], h).astype(BF16)


def _diff_pre_kernel(x_ref, g_ref, win_ref, wk_ref, q_out, k_out, v_out, gate_out, qnorm_out,
                     knorm_out, *, x_token_major):
    x = _feature_major(x_ref, x_token_major)
    t = x.shape[1]
    h = _rms_rows(x, g_ref[...]).astype(BF16)
    scale = DIFF_HD ** -0.5 * LOG2_E
    width = DIFF_HEADS * 2 * DIFF_HD
    q = _dot(win_ref[0:width, :], h) * scale
    zeros_q = jnp.zeros((DK - DIFF_HD, t), F32)
    worst = None
    for hc in range(2 * DIFF_HEADS):
        blk = q[hc * DIFF_HD:(hc + 1) * DIFF_HD]
        parts = [blk, zeros_q] if hc % 2 == 0 else [zeros_q, blk]
        worst = _store_q_block(jnp.concatenate(parts, axis=0), q_out, hc, worst)
    _store_norm_tile(worst, qnorm_out)

    k_nat = lax.dot_general(h, wk_ref[...], (((0,), (0,)), ((), ())),
                            preferred_element_type=F32)
    _store_k_blocks(k_nat, k_out, knorm_out)

    v = _dot(win_ref[width:2 * width, :], h)
    dv = 2 * DIFF_HD
    ones = jnp.ones((ONES_ROWS, t), BF16)
    for hd in range(DIFF_HEADS):
        v_out[0, hd, 0, 0:dv, :] = v[hd * dv:(hd + 1) * dv].astype(BF16)
        v_out[0, hd, 0, dv:dv + ONES_ROWS, :] = ones

    gate_out[0] = _dot(win_ref[2 * width:3 * width, :], h).astype(BF16)


def _attend(q_ts, k_chunk, v_chunk, nkc, dv, scratch, bias_fn=None):
    n = len(q_ts)
    tq = q_ts[0].shape[1]
    s_slots, p_slots = scratch[0:2], scratch[2:4]
    dvx = dv + ONES_ROWS
    assert nkc >= 2 and nkc % 2 == 0

    def qk(c, s_ref):
        maxima = []
        for g in range(n):
            s = _dot(k_chunk(g, c), q_ts[g])
            if bias_fn is not None:
                s = s + bias_fn(g, c)
            s_ref[g] = s
            maxima.append(jnp.max(s, axis=0, keepdims=True))
        return tuple(maxima)

    def softmax(s_ref, s_max, p_ref, m):
        m_out, alpha = [], []
        for g in range(n):
            m_new = jnp.maximum(m[g], s_max[g])
            alpha.append(jnp.exp2(m[g] - m_new))
            p_ref[g] = jnp.exp2(s_ref[g] - m_new).astype(BF16)
            m_out.append(m_new)
        return tuple(m_out), tuple(alpha)

    def pv(c, p_ref, alpha, acc):
        return tuple(alpha[g] * acc[g] + _dot(v_chunk(g, c), p_ref[g]) for g in range(n))

    m = tuple(jnp.full((1, tq), -jnp.inf, F32) for _ in range(n))
    acc = tuple(jnp.zeros((dvx, tq), F32) for _ in range(n))
    max0 = qk(0, s_slots[0])
    max1 = qk(1, s_slots[1])
    m, alpha = softmax(s_slots[0], max0, p_slots[0], m)

    def pair(j, carry):
        m, alpha, acc, max1 = carry
        t = 2 * j + 1
        max0 = qk(t + 1, s_slots[0])
        acc = pv(t - 1, p_slots[0], alpha, acc)
        m, alpha = softmax(s_slots[1], max1, p_slots[1], m)
        max1 = qk(t + 2, s_slots[1])
        acc = pv(t, p_slots[1], alpha, acc)
        m, alpha = softmax(s_slots[0], max0, p_slots[0], m)
        return m, alpha, acc, max1

    m, alpha, acc, max1 = lax.fori_loop(0, (nkc - 2) // 2, pair, (m, alpha, acc, max1))
    acc = pv(nkc - 2, p_slots[0], alpha, acc)
    m, alpha = softmax(s_slots[1], max1, p_slots[1], m)
    acc = pv(nkc - 1, p_slots[1], alpha, acc)
    return [a[0:dv] * (1.0 / a[dv:dv + 1]) for a in acc]


def _attend_bounded(q_ts, k_chunk, v_chunk, nkc, dv, scratch, bias_fn=None):
    n = len(q_ts)
    p_slots = scratch[2:4]
    tk, tq = p_slots[0].shape[1:]
    valu_den = bias_fn is None
    v_rows = dv if valu_den else dv + ONES_ROWS
    col_tiles = [slice(lo, lo + MXU_COLS) for lo in range(0, tq, MXU_COLS)]
    row_tiles = [slice(lo, lo + MXU_ROWS) for lo in range(0, tk, MXU_ROWS)]

    def stage(t, acc, den):
        p_new = p_slots[t % 2] if t < nkc else None
        p_old = p_slots[(t - 1) % 2] if t > 0 else None
        ks = [None if p_new is None else k_chunk(g, t) for g in range(n)]
        vs = [None if p_old is None else v_chunk(g, t - 1)[0:v_rows] for g in range(n)]
        biases = [None if (bias_fn is None or p_new is None) else bias_fn(g, t)
                  for g in range(n)]
        parts = [[acc[g][:, cols] for cols in col_tiles] for g in range(n)]
        dens = [[den[g][:, cols] for cols in col_tiles] for g in range(n)]
        for rows in row_tiles:
            for ci, cols in enumerate(col_tiles):
                for g in range(n):
                    if p_new is not None:
                        s = _dot(ks[g][rows], q_ts[g][:, cols])
                        if biases[g] is not None:
                            s = s + biases[g][rows, cols]
                        p = jnp.exp2(s)
                        p_new[g, rows, cols] = p.astype(BF16)
                        if valu_den:
                            dens[g][ci] = dens[g][ci] + jnp.sum(
                                p.reshape(-1, SUBLANES, MXU_COLS), axis=0)
                    if p_old is not None:
                        parts[g][ci] = parts[g][ci] + _dot(vs[g][:, rows], p_old[g, rows, cols])
        return (tuple(jnp.concatenate(parts[g], axis=1) for g in range(n)),
                tuple(jnp.concatenate(dens[g], axis=1) for g in range(n)))

    acc = tuple(jnp.zeros((v_rows, tq), F32) for _ in range(n))
    den = tuple(jnp.zeros((SUBLANES, tq), F32) for _ in range(n))
    for t in range(nkc + 1):
        acc, den = stage(t, acc, den)
    if valu_den:
        return [a * (1.0 / jnp.sum(d, axis=0, keepdims=True)) for a, d in zip(acc, den)]
    return [a[0:dv] * (1.0 / a[dv:dv + 1]) for a in acc]


def _key_rows(c, tk):
    if isinstance(c, int):
        return pl.ds(c * tk, tk)
    return pl.ds(pl.multiple_of(c * tk, tk), tk)


def _attend_dispatch(q_ts, k_chunk, v_chunk, norm_refs, nkc, dv, scratch, finish, bias_fn=None):
    qn_ref, kn_ref = norm_refs
    flag_ref = scratch[4]

    @pl.when((pl.program_id(1) == 0) & (pl.program_id(2) == 0))
    def _():
        small = jnp.max(qn_ref[0]) * jnp.max(kn_ref[0]) <= SCORE_LIMIT * SCORE_LIMIT
        flag_ref[0] = small.astype(jnp.int32)

    bounded = flag_ref[0] == 1

    @pl.when(bounded)
    def _():
        finish(_attend_bounded(q_ts, k_chunk, v_chunk, nkc, dv, scratch, bias_fn))

    @pl.when(jnp.logical_not(bounded))
    def _():
        finish(_attend(q_ts, k_chunk, v_chunk, nkc, dv, scratch, bias_fn))


def _mla_attn_kernel(q_ref, k_ref, v_ref, qn2_ref, kn2_ref, o_ref, *scratch):
    nkc, tk = v_ref.shape[2], v_ref.shape[4]

    def finish(outs):
        for g, o in enumerate(outs):
            o_ref[0, g * MLA_V:(g + 1) * MLA_V, :] = o.astype(BF16)

    _attend_dispatch([q_ref[0, g] for g in range(ATTN_STREAMS)],
                     lambda g, c: k_ref[0, g, _key_rows(c, tk), :],
                     lambda g, c: v_ref[0, g, c], (qn2_ref, kn2_ref), nkc, MLA_V, scratch,
                     finish)


def _gqa_attn_kernel(q_ref, k_ref, v_ref, qn2_ref, kn2_ref, o_ref, *scratch):
    nkc, tk = v_ref.shape[2], v_ref.shape[4]

    def finish(outs):
        for g, o in enumerate(outs):
            o_ref[0, g * GQA_HD:(g + 1) * GQA_HD, :] = o.astype(BF16)

    _attend_dispatch([q_ref[0, g] for g in range(ATTN_STREAMS)],
                     lambda g, c: k_ref[0, 0, _key_rows(c, tk), :],
                     lambda g, c: v_ref[0, 0, c], (qn2_ref, kn2_ref), nkc, GQA_HD, scratch,
                     finish)


def _diff_attn_kernel(q_ref, k_ref, v_ref, qn2_ref, kn2_ref, lq1_ref, lk1_ref, lq2_ref, lk2_ref,
                      subln_ref, o_ref, *scratch, lambda_init):
    tq = q_ref.shape[3]
    nkc, tk = v_ref.shape[2], v_ref.shape[4]
    heads = ATTN_STREAMS // 2
    dv = 2 * DIFF_HD
    lam = (jnp.exp(jnp.sum(lq1_ref[...] * lk1_ref[...], axis=1, keepdims=True))
           - jnp.exp(jnp.sum(lq2_ref[...] * lk2_ref[...], axis=1, keepdims=True))
           + lambda_init)
    q_pos = (pl.program_id(2) * tq + lax.broadcasted_iota(jnp.int32, (1, tq), 1)).astype(F32)
    k_iota = lax.broadcasted_iota(jnp.int32, (tk, 1), 0)
    slopes = []
    for hd in range(heads):
        exponent = jnp.full((1, 1), 126 - (pl.program_id(1) * heads + hd), jnp.int32)
        slopes.append(lax.bitcast_convert_type(exponent << 23, F32) * LOG2_E)

    def bias_fn(g, c):
        k_pos = (c * tk + k_iota).astype(F32)
        return -slopes[g // 2] * jnp.abs(q_pos - k_pos)

    def finish(outs):
        for hd in range(heads):
            o = outs[2 * hd] - lam * outs[2 * hd + 1]
            o = _rms_rows(o, subln_ref[...]) * (1.0 - lambda_init)
            o_ref[0, hd * dv:(hd + 1) * dv, :] = o.astype(BF16)

    _attend_dispatch([q_ref[0, g] for g in range(ATTN_STREAMS)],
                     lambda g, c: k_ref[0, g // 2, _key_rows(c, tk), :],
                     lambda g, c: v_ref[0, g // 2, c], (qn2_ref, kn2_ref), nkc, dv, scratch, finish,
                     bias_fn)


def _post_kernel(o_ref, gate_ref, x_ref, wout_ref, g_ref, y_ref, *, x_token_major,
                 y_token_major):
    gate = gate_ref[0].astype(F32)
    og = (o_ref[0].astype(F32) * (gate / (1.0 + jnp.exp(-gate)))).astype(BF16)
    m = _dot(wout_ref[...], og)
    y = _feature_major(x_ref, x_token_major) + _rms_rows(m, g_ref[...])
    y_ref[0] = y.T if y_token_major else y


def _params(n_axes):
    return pltpu.CompilerParams(dimension_semantics=("arbitrary",) * n_axes,
                                vmem_limit_bytes=VMEM_LIMIT_BYTES)


def _const_spec(arr):
    nd = arr.ndim
    return pl.BlockSpec(arr.shape, lambda *_: (0,) * nd)


def _col(v):
    return v.astype(F32).reshape(-1, 1)


def _tiles(s):
    tok = min(TOK_TILE, s)
    tq = min(Q_TILE, s)
    assert s % tok == 0 and s % tq == 0
    return tok, tq


def _stream_dims(x, token_major):
    return (x.shape[0], x.shape[1]) if token_major else (x.shape[0], x.shape[2])


def _stream_spec(tok, token_major):
    if token_major:
        return pl.BlockSpec((1, tok, D_MODEL), lambda i, t: (i, t, 0))
    return pl.BlockSpec((1, D_MODEL, tok), lambda i, t: (i, 0, t))


def _pre_call(kernel, name, x, x_token_major, consts, tables, n_heads, n_kblocks, n_vheads, dv):
    b, s = _stream_dims(x, x_token_major)
    tok, _ = _tiles(s)
    dvx = dv + ONES_ROWS
    in_specs = ([_stream_spec(tok, x_token_major)]
                + [_const_spec(c) for c in consts]
                + [pl.BlockSpec((tb.shape[0], tok), lambda i, t: (0, t)) for tb in tables])
    norm_shape = jax.ShapeDtypeStruct((b, s // tok, SUBLANES, LANES), F32)
    norm_spec = pl.BlockSpec((1, 1, SUBLANES, LANES), lambda i, t: (i, t, 0, 0))
    out_shape = (jax.ShapeDtypeStruct((b, n_heads, DK, s), BF16),
                 jax.ShapeDtypeStruct((b, n_kblocks, s, DK), BF16),
                 jax.ShapeDtypeStruct((b, n_vheads, s // tok, dvx, tok), BF16),
                 jax.ShapeDtypeStruct((b, D_MODEL, s), BF16),
                 norm_shape, norm_shape)
    out_specs = (pl.BlockSpec((1, n_heads, DK, tok), lambda i, t: (i, 0, 0, t)),
                 pl.BlockSpec((1, n_kblocks, tok, DK), lambda i, t: (i, 0, t, 0)),
                 pl.BlockSpec((1, n_vheads, 1, dvx, tok), lambda i, t: (i, 0, t, 0, 0)),
                 pl.BlockSpec((1, D_MODEL, tok), lambda i, t: (i, 0, t)),
                 norm_spec, norm_spec)
    return pl.pallas_call(functools.partial(kernel, x_token_major=x_token_major),
                          out_shape=out_shape, grid=(b, s // tok), in_specs=in_specs,
                          out_specs=out_specs, compiler_params=_params(2), name=name)(
                              x, *consts, *tables)


def _attn_call(kernel, name, pre_outs, extra, k_blocks, k_block_of_group, v_heads):
    q_t, k, v_t, _, q_norm2, k_norm2 = pre_outs
    b, n_heads, _, s = q_t.shape
    tok, tq = _tiles(s)
    assert tq == tok
    nkc, dvx, tk = v_t.shape[2:]
    n_groups = n_heads // ATTN_STREAMS
    assert n_groups * v_heads == v_t.shape[1]
    in_specs = ([pl.BlockSpec((1, ATTN_STREAMS, DK, tq), lambda i, g, qi: (i, g, 0, qi)),
                 pl.BlockSpec((1, k_blocks, s, DK),
                              lambda i, g, qi: (i, k_block_of_group(g), 0, 0)),
                 pl.BlockSpec((1, v_heads, nkc, dvx, tk), lambda i, g, qi: (i, g, 0, 0, 0)),
                 pl.BlockSpec((1,) + q_norm2.shape[1:], lambda i, g, qi: (i, 0, 0, 0)),
                 pl.BlockSpec((1,) + k_norm2.shape[1:], lambda i, g, qi: (i, 0, 0, 0))]
                + [_const_spec(e) for e in extra])
    scratch = ([pltpu.VMEM((ATTN_STREAMS, tk, tq), F32)] * 2
               + [pltpu.VMEM((ATTN_STREAMS, tk, tq), BF16)] * 2
               + [pltpu.SMEM((1,), jnp.int32)])
    return pl.pallas_call(
        kernel, out_shape=jax.ShapeDtypeStruct((b, D_MODEL, s), BF16),
        grid=(b, n_groups, s // tq), in_specs=in_specs,
        out_specs=pl.BlockSpec((1, D_MODEL // n_groups, tq), lambda i, g, qi: (i, g, qi)),
        scratch_shapes=scratch, compiler_params=_params(3), name=name)(
            q_t, k, v_t, q_norm2, k_norm2, *extra)


def _post_call(o_t, gate_t, x, x_token_major, y_token_major, wout_t, post_gain):
    b, s = _stream_dims(x, x_token_major)
    tok, _ = _tiles(s)
    tile = _stream_spec(tok, False)
    consts = (wout_t, _col(post_gain))
    y_shape = (b, s, D_MODEL) if y_token_major else (b, D_MODEL, s)
    return pl.pallas_call(
        functools.partial(_post_kernel, x_token_major=x_token_major,
                          y_token_major=y_token_major),
        out_shape=jax.ShapeDtypeStruct(y_shape, F32), grid=(b, s // tok),
        in_specs=([tile, tile, _stream_spec(tok, x_token_major)]
                  + [_const_spec(c) for c in consts]),
        out_specs=_stream_spec(tok, y_token_major), compiler_params=_params(2), name="post")(
            o_t, gate_t, x, *consts)


def _rope_tables(pos, dim, theta):
    inv = 1.0 / (theta ** (jnp.arange(0, dim, 2, dtype=F32) / dim))
    ang = pos[:, None] * inv[None, :]
    return jnp.cos(ang).T, jnp.sin(ang).T


def _mla_layer(x, p, x_token_major=False, y_token_major=False):
    _, s = _stream_dims(x, x_token_major)
    w_ukv = p["w_ukv"].reshape(MLA_KV_LORA, MLA_HEADS, MLA_NOPE + MLA_V)
    wv_t = w_ukv[:, :, MLA_NOPE:].reshape(MLA_KV_LORA, MLA_HEADS * MLA_V).T
    wk = jnp.zeros((2 * LANES, MLA_HEADS, DK), F32)
    wk = wk.at[0:MLA_KV_LORA, :, 0:MLA_NOPE].set(w_ukv[:, :, 0:MLA_NOPE])
    eye = jnp.broadcast_to(jnp.eye(MLA_ROPE, dtype=F32)[:, None, :], (MLA_ROPE, MLA_HEADS, MLA_ROPE))
    wk = wk.at[LANES:LANES + MLA_ROPE, :, MLA_NOPE:MLA_QK].set(eye)
    consts = (_col(p["pre_norm"]), p["w_in"].T.astype(BF16), _col(p["q_norm"]),
              p["w_uq"].T.astype(BF16), _col(p["kv_norm"]), wv_t.astype(BF16),
              wk.reshape(2 * LANES, MLA_HEADS * DK).astype(BF16))
    tables = _rope_tables(jnp.arange(s, dtype=F32), MLA_ROPE, MLA_ROPE_THETA)
    pre = _pre_call(_mla_pre_kernel, "mla_pre", x, x_token_major, consts, tables,
                    MLA_HEADS, MLA_HEADS, MLA_HEADS, MLA_V)
    o_t = _attn_call(_mla_attn_kernel, "mla_attn", pre, (), ATTN_STREAMS, lambda g: g,
                     ATTN_STREAMS)
    return _post_call(o_t, pre[3], x, x_token_major, y_token_major, p["w_out"].T.astype(BF16),
                      p["post_norm"])


def _gqa_layer(x, p, x_token_major=False, y_token_major=False):
    _, s = _stream_dims(x, x_token_major)
    consts = (_col(p["pre_norm"]), p["w_in"].T.astype(BF16), _col(p["q_norm"]), _col(p["k_norm"]))
    t_idx = jnp.arange(s)
    half = GQA_HD // 2
    tables = (_rope_tables((t_idx // GRID_W).astype(F32), half, GQA_ROPE_THETA)
              + _rope_tables((t_idx % GRID_W).astype(F32), half, GQA_ROPE_THETA))
    pre = _pre_call(_gqa_pre_kernel, "gqa_pre", x, x_token_major, consts, tables, GQA_HEADS,
                    GQA_KV_HEADS * GQA_HD // DK, GQA_KV_HEADS, GQA_HD)
    assert ATTN_STREAMS == GQA_GROUP
    o_t = _attn_call(_gqa_attn_kernel, "gqa_attn", pre, (), 1, lambda g: g // 2, 1)
    return _post_call(o_t, pre[3], x, x_token_major, y_token_major, p["w_out"].T.astype(BF16),
                      p["post_norm"])


def _diff_layer(x, p, layer_idx, x_token_major=False, y_token_major=False):
    lambda_init = 0.8 - 0.6 * math.exp(-0.3 * layer_idx)
    width = DIFF_HEADS * 2 * DIFF_HD
    w_in = p["w_in"]
    w_qvg = jnp.concatenate([w_in[:, 0:width], w_in[:, 2 * width:4 * width]], axis=1)
    consts = (_col(p["pre_norm"]), w_qvg.T.astype(BF16), w_in[:, width:2 * width].astype(BF16))
    pre = _pre_call(_diff_pre_kernel, "diff_pre", x, x_token_major, consts, (),
                    2 * DIFF_HEADS, DIFF_HEADS, DIFF_HEADS, 2 * DIFF_HD)
    row = lambda v: v.astype(F32).reshape(1, -1)
    extra = (row(p["lambda_q1"]), row(p["lambda_k1"]), row(p["lambda_q2"]), row(p["lambda_k2"]),
             _col(p["subln"]))
    o_t = _attn_call(functools.partial(_diff_attn_kernel, lambda_init=lambda_init), "diff_attn",
                     pre, extra, ATTN_STREAMS // 2, lambda g: g, ATTN_STREAMS // 2)
    return _post_call(o_t, pre[3], x, x_token_major, y_token_major, p["w_out"].T.astype(BF16),
                      p["post_norm"])


def _trunk(x, layers):
    last = len(layers) - 1
    for i, p in enumerate(layers):
        kind = i % 3
        layout = dict(x_token_major=(i == 0), y_token_major=(i == last))
        if kind == 0:
            x = _mla_layer(x, p, **layout)
        elif kind == 1:
            x = _gqa_layer(x, p, **layout)
        else:
            x = _diff_layer(x, p, i, **layout)
    return x


def kernel(x_prompt, x_sample, l0_pre_norm, l0_w_in, l0_q_norm, l0_w_uq, l0_kv_norm, l0_w_ukv, l0_w_out, l0_post_norm, l1_pre_norm, l1_w_in, l1_q_norm, l1_k_norm, l1_w_out, l1_post_norm, l2_pre_norm, l2_w_in, l2_lambda_q1, l2_lambda_k1, l2_lambda_q2, l2_lambda_k2, l2_subln, l2_w_out, l2_post_norm, l3_pre_norm, l3_w_in, l3_q_norm, l3_w_uq, l3_kv_norm, l3_w_ukv, l3_w_out, l3_post_norm):
    layers = [
        dict(pre_norm=l0_pre_norm, w_in=l0_w_in, q_norm=l0_q_norm, w_uq=l0_w_uq,
             kv_norm=l0_kv_norm, w_ukv=l0_w_ukv, w_out=l0_w_out, post_norm=l0_post_norm),
        dict(pre_norm=l1_pre_norm, w_in=l1_w_in, q_norm=l1_q_norm, k_norm=l1_k_norm,
             w_out=l1_w_out, post_norm=l1_post_norm),
        dict(pre_norm=l2_pre_norm, w_in=l2_w_in, lambda_q1=l2_lambda_q1, lambda_k1=l2_lambda_k1,
             lambda_q2=l2_lambda_q2, lambda_k2=l2_lambda_k2, subln=l2_subln,
             w_out=l2_w_out, post_norm=l2_post_norm),
        dict(pre_norm=l3_pre_norm, w_in=l3_w_in, q_norm=l3_q_norm, w_uq=l3_w_uq,
             kv_norm=l3_kv_norm, w_ukv=l3_w_ukv, w_out=l3_w_out, post_norm=l3_post_norm),
    ]
    return (_trunk(x_prompt, layers), _trunk(x_sample, layers))
```

```python
import functools
import math

import jax
import jax.numpy as jnp
from jax import lax
from jax.experimental import pallas as pl
from jax.experimental.pallas import tpu as pltpu

F32 = jnp.float32
BF16 = jnp.bfloat16

D_MODEL = 1024
NORM_EPS = 1e-6
GRID_W = 64

MLA_HEADS = 16
MLA_Q_LORA = 256
MLA_KV_LORA = 128
MLA_NOPE = 64
MLA_ROPE = 32
MLA_V = 64
MLA_QK = MLA_NOPE + MLA_ROPE
MLA_ROPE_THETA = 10000.0

GQA_HEADS = 16
GQA_KV_HEADS = 4
GQA_GROUP = GQA_HEADS // GQA_KV_HEADS
GQA_HD = 64
GQA_ROPE_THETA = 10000.0

DIFF_HEADS = 8
DIFF_HD = 64

LOG2_E = math.log2(math.e)

LANES = 128
BF16_ROWS = 16
DK = LANES
ONES_ROWS = BF16_ROWS
TOK_TILE = 512
POST_TILE = 1024
Q_TILE = 512
ATTN_STREAMS = 4
SCORE_LIMIT = 60.0
SUBLANES = 8
MXU_ROWS = 256
MXU_COLS = 256
VMEM_LIMIT_BYTES = 56 * 1024 * 1024


def _rms_rows(x, gain_col):
    ms = jnp.mean(x * x, axis=0, keepdims=True)
    return x * lax.rsqrt(ms + NORM_EPS) * gain_col


def _rope_rows(x1, x2, cos, sin):
    return x1 * cos - x2 * sin, x1 * sin + x2 * cos


def _dot(a, b):
    return jnp.dot(a, b, preferred_element_type=F32)


def _feature_major(x_ref, token_major):
    return x_ref[0].T if token_major else x_ref[0]


def _store_norm_tile(worst, norm_out):
    top = jnp.max(jnp.max(worst, axis=0, keepdims=True), axis=1, keepdims=True)
    norm_out[0, 0] = jnp.broadcast_to(top, norm_out.shape[2:])


def _store_q_block(q_f32, q_out, head, worst, rows):
    qb = q_f32.astype(BF16)
    q_out[0, head] = qb
    qf = qb[rows].astype(F32)
    n2 = jnp.sum(qf * qf, axis=0, keepdims=True)
    return n2 if worst is None else jnp.maximum(worst, n2)


def _store_k_blocks(k_nat, k_out, knorm_out):
    worst = None
    for j in range(k_out.shape[1]):
        kb = k_nat[:, j * DK:(j + 1) * DK].astype(BF16)
        k_out[0, j] = kb
        kf = kb.astype(F32)
        n2 = jnp.sum(kf * kf, axis=1, keepdims=True)
        worst = n2 if worst is None else jnp.maximum(worst, n2)
    _store_norm_tile(worst, knorm_out)


def _mla_pre_kernel(x_ref, g_ref, win_ref, qn_ref, wuq_ref, kvn_ref, wv_ref, wk_ref,
                    cos_ref, sin_ref, q_out, k_out, v_out, gate_out, qnorm_out, knorm_out,
                    *, x_token_major):
    x = _feature_major(x_ref, x_token_major)
    t = x.shape[1]
    h = _rms_rows(x, g_ref[...]).astype(BF16)
    cos = cos_ref[...]
    sin = sin_ref[...]
    scale = MLA_QK ** -0.5 * LOG2_E

    cq = _dot(win_ref[0:MLA_Q_LORA, :], h)
    cqn = _rms_rows(cq, qn_ref[...]).astype(BF16)
    q_all = _dot(wuq_ref[...], cqn) * scale
    zeros_q = jnp.zeros((DK - MLA_QK, t), F32)
    half = MLA_ROPE // 2
    worst = None
    for hd in range(MLA_HEADS):
        base = hd * MLA_QK
        nope = q_all[base:base + MLA_NOPE]
        r1, r2 = _rope_rows(q_all[base + MLA_NOPE:base + MLA_NOPE + half],
                            q_all[base + MLA_NOPE + half:base + MLA_QK], cos, sin)
        worst = _store_q_block(jnp.concatenate([nope, r1, r2, zeros_q], axis=0), q_out, hd, worst,
                               slice(0, MLA_QK))
    _store_norm_tile(worst, qnorm_out)

    kv_lo = MLA_Q_LORA
    kv_hi = MLA_Q_LORA + MLA_KV_LORA + MLA_ROPE
    ckr = _dot(win_ref[kv_lo:kv_hi, :], h)
    ckvn = _rms_rows(ckr[0:MLA_KV_LORA], kvn_ref[...])
    v_all = _dot(wv_ref[...], ckvn.astype(BF16))
    ones = jnp.ones((ONES_ROWS, t), BF16)
    for hd in range(MLA_HEADS):
        v_out[0, hd, 0, 0:MLA_V, :] = v_all[hd * MLA_V:(hd + 1) * MLA_V].astype(BF16)
        v_out[0, hd, 0, MLA_V:MLA_V + ONES_ROWS, :] = ones

    kr = ckr[MLA_KV_LORA:MLA_KV_LORA + MLA_ROPE]
    r1, r2 = _rope_rows(kr[0:half], kr[half:MLA_ROPE], cos, sin)
    krr = jnp.concatenate([r1, r2, jnp.zeros((LANES - MLA_ROPE, t), F32)], axis=0)
    z = jnp.concatenate([ckvn.T, krr.T], axis=1).astype(BF16)
    k_all = _dot(z, wk_ref[...])
    _store_k_blocks(k_all, k_out, knorm_out)

    gate_out[0] = _dot(win_ref[kv_hi:kv_hi + D_MODEL, :], h).astype(BF16)


def _gqa_pre_kernel(x_ref, g_ref, win_ref, qn_ref, kn_ref, cr_ref, sr_ref, cc_ref, sc_ref,
                    q_out, k_out, v_out, gate_out, qnorm_out, knorm_out, *, x_token_major):
    x = _feature_major(x_ref, x_token_major)
    t = x.shape[1]
    h = _rms_rows(x, g_ref[...]).astype(BF16)
    cr, sr, cc, sc = cr_ref[...], sr_ref[...], cc_ref[...], sc_ref[...]
    scale = GQA_HD ** -0.5 * LOG2_E
    n_q = GQA_HEADS * GQA_HD
    n_kv = GQA_KV_HEADS * GQA_HD
    qk = _dot(win_ref[0:n_q + n_kv, :], h)
    zeros_q = jnp.zeros((DK - GQA_HD, t), F32)
    quarter = GQA_HD // 4

    def norm_rope(blk, gain_col):
        xn = _rms_rows(blk, gain_col)
        a, b = _rope_rows(xn[0:quarter], xn[quarter:2 * quarter], cr, sr)
        c, d = _rope_rows(xn[2 * quarter:3 * quarter], xn[3 * quarter:4 * quarter], cc, sc)
        return jnp.concatenate([a, b, c, d], axis=0)

    worst = None
    for hd in range(GQA_HEADS):
        rq = norm_rope(qk[hd * GQA_HD:(hd + 1) * GQA_HD], qn_ref[...]) * scale
        kv = hd // GQA_GROUP
        parts = [rq, zeros_q] if kv % 2 == 0 else [zeros_q, rq]
        lo = (kv % 2) * GQA_HD
        worst = _store_q_block(jnp.concatenate(parts, axis=0), q_out, hd, worst,
                               slice(lo, lo + GQA_HD))
    _store_norm_tile(worst, qnorm_out)

    k_rows = [norm_rope(qk[n_q + kv * GQA_HD:n_q + (kv + 1) * GQA_HD], kn_ref[...])
              for kv in range(GQA_KV_HEADS)]
    k_nat = jnp.concatenate(k_rows, axis=0).T
    _store_k_blocks(k_nat, k_out, knorm_out)

    v = _dot(win_ref[n_q + n_kv:n_q + 2 * n_kv, :], h)
    ones = jnp.ones((ONES_ROWS, t), BF16)
    for kv in range(GQA_KV_HEADS):
        v_out[0, kv, 0, 0:GQA_HD, :] = v[kv * GQA_HD:(kv + 1) * GQA_HD].astype(BF16)
        v_out[0, kv, 0, GQA_HD:GQA_HD + ONES_ROWS, :] = ones

    g_lo = n_q + 2 * n_kv
    gate_out[0] = _dot(win_ref[g_lo:g_lo + D_MODEL, :], h).astype(BF16)


def _diff_pre_kernel(x_ref, g_ref, win_ref, wk_ref, q_out, k_out, v_out, gate_out, qnorm_out,
                     knorm_out, *, x_token_major):
    x = _feature_major(x_ref, x_token_major)
    t = x.shape[1]
    h = _rms_rows(x, g_ref[...]).astype(BF16)
    scale = DIFF_HD ** -0.5 * LOG2_E
    width = DIFF_HEADS * 2 * DIFF_HD
    q = _dot(win_ref[0:width, :], h) * scale
    zeros_q = jnp.zeros((DK - DIFF_HD, t), F32)
    worst = None
    for hc in range(2 * DIFF_HEADS):
        blk = q[hc * DIFF_HD:(hc + 1) * DIFF_HD]
        parts = [blk, zeros_q] if hc % 2 == 0 else [zeros_q, blk]
        lo = (hc % 2) * DIFF_HD
        worst = _store_q_block(jnp.concatenate(parts, axis=0), q_out, hc, worst,
                               slice(lo, lo + DIFF_HD))
    _store_norm_tile(worst, qnorm_out)

    k_nat = lax.dot_general(h, wk_ref[...], (((0,), (0,)), ((), ())),
                            preferred_element_type=F32)
    _store_k_blocks(k_nat, k_out, knorm_out)

    v = _dot(win_ref[width:2 * width, :], h)
    dv = 2 * DIFF_HD
    ones = jnp.ones((ONES_ROWS, t), BF16)
    for hd in range(DIFF_HEADS):
        v_out[0, hd, 0, 0:dv, :] = v[hd * dv:(hd + 1) * dv].astype(BF16)
        v_out[0, hd, 0, dv:dv + ONES_ROWS, :] = ones

    gate_out[0] = _dot(win_ref[2 * width:3 * width, :], h).astype(BF16)


def _attend(q_ts, k_chunk, v_chunk, nkc, dv, scratch, bias_fn=None):
    n = len(q_ts)
    tq = q_ts[0].shape[1]
    s_slots, p_slots = scratch[0:2], scratch[2:4]
    dvx = dv + ONES_ROWS
    assert nkc >= 2 and nkc % 2 == 0

    def qk(c, s_ref):
        maxima = []
        for g in range(n):
            s = _dot(k_chunk(g, c), q_ts[g])
            if bias_fn is not None:
                s = s + bias_fn(g, c)
            s_ref[g] = s
            maxima.append(jnp.max(s, axis=0, keepdims=True))
        return tuple(maxima)

    def softmax(s_ref, s_max, p_ref, m):
        m_out, alpha = [], []
        for g in range(n):
            m_new = jnp.maximum(m[g], s_max[g])
            alpha.append(jnp.exp2(m[g] - m_new))
            p_ref[g] = jnp.exp2(s_ref[g] - m_new).astype(BF16)
            m_out.append(m_new)
        return tuple(m_out), tuple(alpha)

    def pv(c, p_ref, alpha, acc):
        return tuple(alpha[g] * acc[g] + _dot(v_chunk(g, c), p_ref[g]) for g in range(n))

    m = tuple(jnp.full((1, tq), -jnp.inf, F32) for _ in range(n))
    acc = tuple(jnp.zeros((dvx, tq), F32) for _ in range(n))
    max0 = qk(0, s_slots[0])
    max1 = qk(1, s_slots[1])
    m, alpha = softmax(s_slots[0], max0, p_slots[0], m)

    def pair(j, carry):
        m, alpha, acc, max1 = carry
        t = 2 * j + 1
        max0 = qk(t + 1, s_slots[0])
        acc = pv(t - 1, p_slots[0], alpha, acc)
        m, alpha = softmax(s_slots[1], max1, p_slots[1], m)
        max1 = qk(t + 2, s_slots[1])
        acc = pv(t, p_slots[1], alpha, acc)
        m, alpha = softmax(s_slots[0], max0, p_slots[0], m)
        return m, alpha, acc, max1

    m, alpha, acc, max1 = lax.fori_loop(0, (nkc - 2) // 2, pair, (m, alpha, acc, max1))
    acc = pv(nkc - 2, p_slots[0], alpha, acc)
    m, alpha = softmax(s_slots[1], max1, p_slots[1], m)
    acc = pv(nkc - 1, p_slots[1], alpha, acc)
    return [a[0:dv] * (1.0 / a[dv:dv + 1]) for a in acc]


def _attend_bounded(q_ts, k_chunk, v_chunk, nkc, dv, scratch, bias_fn=None):
    n = len(q_ts)
    p_slots = scratch[2:4]
    tk, tq = p_slots[0].shape[1:]
    valu_den = bias_fn is None
    v_rows = dv if valu_den else dv + ONES_ROWS
    col_tiles = [slice(lo, lo + MXU_COLS) for lo in range(0, tq, MXU_COLS)]
    row_tiles = [slice(lo, lo + MXU_ROWS) for lo in range(0, tk, MXU_ROWS)]

    def stage(t, acc, den):
        p_new = p_slots[t % 2] if t < nkc else None
        p_old = p_slots[(t - 1) % 2] if t > 0 else None
        ks = [None if p_new is None else k_chunk(g, t) for g in range(n)]
        vs = [None if p_old is None else v_chunk(g, t - 1)[0:v_rows] for g in range(n)]
        biases = [None if (bias_fn is None or p_new is None) else bias_fn(g, t)
                  for g in range(n)]
        parts = [[acc[g][:, cols] for cols in col_tiles] for g in range(n)]
        dens = [[den[g][:, cols] for cols in col_tiles] for g in range(n)]
        for rows in row_tiles:
            for ci, cols in enumerate(col_tiles):
                for g in range(n):
                    if p_new is not None:
                        s = _dot(ks[g][rows], q_ts[g][:, cols])
                        if biases[g] is not None:
                            s = s + biases[g][rows, cols]
                        p = jnp.exp2(s)
                        p_new[g, rows, cols] = p.astype(BF16)
                        if valu_den:
                            dens[g][ci] = dens[g][ci] + jnp.sum(
                                p.reshape(-1, SUBLANES, MXU_COLS), axis=0)
                    if p_old is not None:
                        parts[g][ci] = parts[g][ci] + _dot(vs[g][:, rows], p_old[g, rows, cols])
        return (tuple(jnp.concatenate(parts[g], axis=1) for g in range(n)),
                tuple(jnp.concatenate(dens[g], axis=1) for g in range(n)))

    acc = tuple(jnp.zeros((v_rows, tq), F32) for _ in range(n))
    den = tuple(jnp.zeros((SUBLANES, tq), F32) for _ in range(n))
    for t in range(nkc + 1):
        acc, den = stage(t, acc, den)
    if valu_den:
        return [a * (1.0 / jnp.sum(d, axis=0, keepdims=True)) for a, d in zip(acc, den)]
    return [a[0:dv] * (1.0 / a[dv:dv + 1]) for a in acc]


def _key_rows(c, tk):
    if isinstance(c, int):
        return pl.ds(c * tk, tk)
    return pl.ds(pl.multiple_of(c * tk, tk), tk)


def _attend_dispatch(q_ts, k_chunk, v_chunk, norm_refs, nkc, dv, scratch, finish, bias_fn=None):
    qn_ref, kn_ref = norm_refs
    flag_ref = scratch[4]

    @pl.when((pl.program_id(1) == 0) & (pl.program_id(2) == 0))
    def _():
        small = jnp.max(qn_ref[0]) * jnp.max(kn_ref[0]) <= SCORE_LIMIT * SCORE_LIMIT
        flag_ref[0] = small.astype(jnp.int32)

    bounded = flag_ref[0] == 1

    @pl.when(bounded)
    def _():
        finish(_attend_bounded(q_ts, k_chunk, v_chunk, nkc, dv, scratch, bias_fn))

    @pl.when(jnp.logical_not(bounded))
    def _():
        finish(_attend(q_ts, k_chunk, v_chunk, nkc, dv, scratch, bias_fn))


def _mla_attn_kernel(q_ref, k_ref, v_ref, qn2_ref, kn2_ref, o_ref, *scratch):
    nkc, tk = v_ref.shape[2], v_ref.shape[4]

    def finish(outs):
        for g, o in enumerate(outs):
            o_ref[0, g * MLA_V:(g + 1) * MLA_V, :] = o.astype(BF16)

    _attend_dispatch([q_ref[0, g] for g in range(ATTN_STREAMS)],
                     lambda g, c: k_ref[0, g, _key_rows(c, tk), :],
                     lambda g, c: v_ref[0, g, c], (qn2_ref, kn2_ref), nkc, MLA_V, scratch,
                     finish)


def _gqa_attn_kernel(q_ref, k_ref, v_ref, qn2_ref, kn2_ref, o_ref, *scratch):
    nkc, tk = v_ref.shape[2], v_ref.shape[4]

    def finish(outs):
        for g, o in enumerate(outs):
            o_ref[0, g * GQA_HD:(g + 1) * GQA_HD, :] = o.astype(BF16)

    _attend_dispatch([q_ref[0, g] for g in range(ATTN_STREAMS)],
                     lambda g, c: k_ref[0, 0, _key_rows(c, tk), :],
                     lambda g, c: v_ref[0, 0, c], (qn2_ref, kn2_ref), nkc, GQA_HD, scratch,
                     finish)


def _diff_attn_kernel(q_ref, k_ref, v_ref, qn2_ref, kn2_ref, lq1_ref, lk1_ref, lq2_ref, lk2_ref,
                      subln_ref, o_ref, *scratch, lambda_init):
    tq = q_ref.shape[3]
    nkc, tk = v_ref.shape[2], v_ref.shape[4]
    heads = ATTN_STREAMS // 2
    dv = 2 * DIFF_HD
    lam = (jnp.exp(jnp.sum(lq1_ref[...] * lk1_ref[...], axis=1, keepdims=True))
           - jnp.exp(jnp.sum(lq2_ref[...] * lk2_ref[...], axis=1, keepdims=True))
           + lambda_init)
    q_pos = (pl.program_id(2) * tq + lax.broadcasted_iota(jnp.int32, (1, tq), 1)).astype(F32)
    k_iota = lax.broadcasted_iota(jnp.int32, (tk, 1), 0)
    slopes = []
    for hd in range(heads):
        exponent = jnp.full((1, 1), 126 - (pl.program_id(1) * heads + hd), jnp.int32)
        slopes.append(lax.bitcast_convert_type(exponent << 23, F32) * LOG2_E)

    def bias_fn(g, c):
        k_pos = (c * tk + k_iota).astype(F32)
        return -slopes[g // 2] * jnp.abs(q_pos - k_pos)

    def finish(outs):
        for hd in range(heads):
            o = outs[2 * hd] - lam * outs[2 * hd + 1]
            o = _rms_rows(o, subln_ref[...]) * (1.0 - lambda_init)
            o_ref[0, hd * dv:(hd + 1) * dv, :] = o.astype(BF16)

    _attend_dispatch([q_ref[0, g] for g in range(ATTN_STREAMS)],
                     lambda g, c: k_ref[0, g // 2, _key_rows(c, tk), :],
                     lambda g, c: v_ref[0, g // 2, c], (qn2_ref, kn2_ref), nkc, dv, scratch, finish,
                     bias_fn)


def _post_kernel(o_ref, gate_ref, x_ref, wout_ref, g_ref, y_ref, *, x_token_major,
                 y_token_major):
    gate = gate_ref[0].astype(F32)
    og = (o_ref[0].astype(F32) * (gate / (1.0 + jnp.exp(-gate)))).astype(BF16)
    m = _dot(wout_ref[...], og)
    y = _feature_major(x_ref, x_token_major) + _rms_rows(m, g_ref[...])
    y_ref[0] = y.T if y_token_major else y


def _params(n_axes):
    return pltpu.CompilerParams(dimension_semantics=("arbitrary",) * n_axes,
                                vmem_limit_bytes=VMEM_LIMIT_BYTES)


def _const_spec(arr):
    nd = arr.ndim
    return pl.BlockSpec(arr.shape, lambda *_: (0,) * nd)


def _col(v):
    return v.astype(F32).reshape(-1, 1)


def _tiles(s):
    tok = min(TOK_TILE, s)
    tq = min(Q_TILE, s)
    assert s % tok == 0 and s % tq == 0
    return tok, tq


def _stream_dims(x, token_major):
    return (x.shape[0], x.shape[1]) if token_major else (x.shape[0], x.shape[2])


def _stream_spec(tok, token_major):
    if token_major:
        return pl.BlockSpec((1, tok, D_MODEL), lambda i, t: (i, t, 0))
    return pl.BlockSpec((1, D_MODEL, tok), lambda i, t: (i, 0, t))


def _pre_call(kernel, name, x, x_token_major, consts, tables, n_heads, n_kblocks, n_vheads, dv):
    b, s = _stream_dims(x, x_token_major)
    tok, _ = _tiles(s)
    dvx = dv + ONES_ROWS
    in_specs = ([_stream_spec(tok, x_token_major)]
                + [_const_spec(c) for c in consts]
                + [pl.BlockSpec((tb.shape[0], tok), lambda i, t: (0, t)) for tb in tables])
    norm_shape = jax.ShapeDtypeStruct((b, s // tok, SUBLANES, LANES), F32)
    norm_spec = pl.BlockSpec((1, 1, SUBLANES, LANES), lambda i, t: (i, t, 0, 0))
    out_shape = (jax.ShapeDtypeStruct((b, n_heads, DK, s), BF16),
                 jax.ShapeDtypeStruct((b, n_kblocks, s, DK), BF16),
                 jax.ShapeDtypeStruct((b, n_vheads, s // tok, dvx, tok), BF16),
                 jax.ShapeDtypeStruct((b, D_MODEL, s), BF16),
                 norm_shape, norm_shape)
    out_specs = (pl.BlockSpec((1, n_heads, DK, tok), lambda i, t: (i, 0, 0, t)),
                 pl.BlockSpec((1, n_kblocks, tok, DK), lambda i, t: (i, 0, t, 0)),
                 pl.BlockSpec((1, n_vheads, 1, dvx, tok), lambda i, t: (i, 0, t, 0, 0)),
                 pl.BlockSpec((1, D_MODEL, tok), lambda i, t: (i, 0, t)),
                 norm_spec, norm_spec)
    return pl.pallas_call(functools.partial(kernel, x_token_major=x_token_major),
                          out_shape=out_shape, grid=(b, s // tok), in_specs=in_specs,
                          out_specs=out_specs, compiler_params=_params(2), name=name)(
                              x, *consts, *tables)


def _attn_call(kernel, name, pre_outs, extra, k_blocks, k_block_of_group, v_heads):
    q_t, k, v_t, _, q_norm2, k_norm2 = pre_outs
    b, n_heads, _, s = q_t.shape
    tok, tq = _tiles(s)
    assert tq == tok
    nkc, dvx, tk = v_t.shape[2:]
    n_groups = n_heads // ATTN_STREAMS
    assert n_groups * v_heads == v_t.shape[1]
    in_specs = ([pl.BlockSpec((1, ATTN_STREAMS, DK, tq), lambda i, g, qi: (i, g, 0, qi)),
                 pl.BlockSpec((1, k_blocks, s, DK),
                              lambda i, g, qi: (i, k_block_of_group(g), 0, 0)),
                 pl.BlockSpec((1, v_heads, nkc, dvx, tk), lambda i, g, qi: (i, g, 0, 0, 0)),
                 pl.BlockSpec((1,) + q_norm2.shape[1:], lambda i, g, qi: (i, 0, 0, 0)),
                 pl.BlockSpec((1,) + k_norm2.shape[1:], lambda i, g, qi: (i, 0, 0, 0))]
                + [_const_spec(e) for e in extra])
    scratch = ([pltpu.VMEM((ATTN_STREAMS, tk, tq), F32)] * 2
               + [pltpu.VMEM((ATTN_STREAMS, tk, tq), BF16)] * 2
               + [pltpu.SMEM((1,), jnp.int32)])
    return pl.pallas_call(
        kernel, out_shape=jax.ShapeDtypeStruct((b, D_MODEL, s), BF16),
        grid=(b, n_groups, s // tq), in_specs=in_specs,
        out_specs=pl.BlockSpec((1, D_MODEL // n_groups, tq), lambda i, g, qi: (i, g, qi)),
        scratch_shapes=scratch, compiler_params=_params(3), name=name)(
            q_t, k, v_t, q_norm2, k_norm2, *extra)


def _post_call(o_t, gate_t, x, x_token_major, y_token_major, wout_t, post_gain):
    b, s = _stream_dims(x, x_token_major)
    tok = min(POST_TILE, s)
    assert s % tok == 0
    tile = _stream_spec(tok, False)
    consts = (wout_t, _col(post_gain))
    y_shape = (b, s, D_MODEL) if y_token_major else (b, D_MODEL, s)
    return pl.pallas_call(
        functools.partial(_post_kernel, x_token_major=x_token_major,
                          y_token_major=y_token_major),
        out_shape=jax.ShapeDtypeStruct(y_shape, F32), grid=(b, s // tok),
        in_specs=([tile, tile, _stream_spec(tok, x_token_major)]
                  + [_const_spec(c) for c in consts]),
        out_specs=_stream_spec(tok, y_token_major), compiler_params=_params(2), name="post")(
            o_t, gate_t, x, *consts)


def _rope_tables(pos, dim, theta):
    inv = 1.0 / (theta ** (jnp.arange(0, dim, 2, dtype=F32) / dim))
    ang = pos[:, None] * inv[None, :]
    return jnp.cos(ang).T, jnp.sin(ang).T


def _mla_layer(x, p, x_token_major=False, y_token_major=False):
    _, s = _stream_dims(x, x_token_major)
    w_ukv = p["w_ukv"].reshape(MLA_KV_LORA, MLA_HEADS, MLA_NOPE + MLA_V)
    wv_t = w_ukv[:, :, MLA_NOPE:].reshape(MLA_KV_LORA, MLA_HEADS * MLA_V).T
    wk = jnp.zeros((2 * LANES, MLA_HEADS, DK), F32)
    wk = wk.at[0:MLA_KV_LORA, :, 0:MLA_NOPE].set(w_ukv[:, :, 0:MLA_NOPE])
    eye = jnp.broadcast_to(jnp.eye(MLA_ROPE, dtype=F32)[:, None, :], (MLA_ROPE, MLA_HEADS, MLA_ROPE))
    wk = wk.at[LANES:LANES + MLA_ROPE, :, MLA_NOPE:MLA_QK].set(eye)
    consts = (_col(p["pre_norm"]), p["w_in"].T.astype(BF16), _col(p["q_norm"]),
              p["w_uq"].T.astype(BF16), _col(p["kv_norm"]), wv_t.astype(BF16),
              wk.reshape(2 * LANES, MLA_HEADS * DK).astype(BF16))
    tables = _rope_tables(jnp.arange(s, dtype=F32), MLA_ROPE, MLA_ROPE_THETA)
    pre = _pre_call(_mla_pre_kernel, "mla_pre", x, x_token_major, consts, tables,
                    MLA_HEADS, MLA_HEADS, MLA_HEADS, MLA_V)
    o_t = _attn_call(_mla_attn_kernel, "mla_attn", pre, (), ATTN_STREAMS, lambda g: g,
                     ATTN_STREAMS)
    return _post_call(o_t, pre[3], x, x_token_major, y_token_major, p["w_out"].T.astype(BF16),
                      p["post_norm"])


def _gqa_layer(x, p, x_token_major=False, y_token_major=False):
    _, s = _stream_dims(x, x_token_major)
    consts = (_col(p["pre_norm"]), p["w_in"].T.astype(BF16), _col(p["q_norm"]), _col(p["k_norm"]))
    t_idx = jnp.arange(s)
    half = GQA_HD // 2
    tables = (_rope_tables((t_idx // GRID_W).astype(F32), half, GQA_ROPE_THETA)
              + _rope_tables((t_idx % GRID_W).astype(F32), half, GQA_ROPE_THETA))
    pre = _pre_call(_gqa_pre_kernel, "gqa_pre", x, x_token_major, consts, tables, GQA_HEADS,
                    GQA_KV_HEADS * GQA_HD // DK, GQA_KV_HEADS, GQA_HD)
    assert ATTN_STREAMS == GQA_GROUP
    o_t = _attn_call(_gqa_attn_kernel, "gqa_attn", pre, (), 1, lambda g: g // 2, 1)
    return _post_call(o_t, pre[3], x, x_token_major, y_token_major, p["w_out"].T.astype(BF16),
                      p["post_norm"])


def _diff_layer(x, p, layer_idx, x_token_major=False, y_token_major=False):
    lambda_init = 0.8 - 0.6 * math.exp(-0.3 * layer_idx)
    width = DIFF_HEADS * 2 * DIFF_HD
    w_in = p["w_in"]
    w_qvg = jnp.concatenate([w_in[:, 0:width], w_in[:, 2 * width:4 * width]], axis=1)
    consts = (_col(p["pre_norm"]), w_qvg.T.astype(BF16), w_in[:, width:2 * width].astype(BF16))
    pre = _pre_call(_diff_pre_kernel, "diff_pre", x, x_token_major, consts, (),
                    2 * DIFF_HEADS, DIFF_HEADS, DIFF_HEADS, 2 * DIFF_HD)
    row = lambda v: v.astype(F32).reshape(1, -1)
    extra = (row(p["lambda_q1"]), row(p["lambda_k1"]), row(p["lambda_q2"]), row(p["lambda_k2"]),
             _col(p["subln"]))
    o_t = _attn_call(functools.partial(_diff_attn_kernel, lambda_init=lambda_init), "diff_attn",
                     pre, extra, ATTN_STREAMS // 2, lambda g: g, ATTN_STREAMS // 2)
    return _post_call(o_t, pre[3], x, x_token_major, y_token_major, p["w_out"].T.astype(BF16),
                      p["post_norm"])


def _trunk(x, layers):
    last = len(layers) - 1
    for i, p in enumerate(layers):
        kind = i % 3
        layout = dict(x_token_major=(i == 0), y_token_major=(i == last))
        if kind == 0:
            x = _mla_layer(x, p, **layout)
        elif kind == 1:
            x = _gqa_layer(x, p, **layout)
        else:
            x = _diff_layer(x, p, i, **layout)
    return x


def kernel(x_prompt, x_sample, l0_pre_norm, l0_w_in, l0_q_norm, l0_w_uq, l0_kv_norm, l0_w_ukv, l0_w_out, l0_post_norm, l1_pre_norm, l1_w_in, l1_q_norm, l1_k_norm, l1_w_out, l1_post_norm, l2_pre_norm, l2_w_in, l2_lambda_q1, l2_lambda_k1, l2_lambda_q2, l2_lambda_k2, l2_subln, l2_w_out, l2_post_norm, l3_pre_norm, l3_w_in, l3_q_norm, l3_w_uq, l3_kv_norm, l3_w_ukv, l3_w_out, l3_post_norm):
    layers = [
        dict(pre_norm=l0_pre_norm, w_in=l0_w_in, q_norm=l0_q_norm, w_uq=l0_w_uq,
             kv_norm=l0_kv_norm, w_ukv=l0_w_ukv, w_out=l0_w_out, post_norm=l0_post_norm),
        dict(pre_norm=l1_pre_norm, w_in=l1_w_in, q_norm=l1_q_norm, k_norm=l1_k_norm,
             w_out=l1_w_out, post_norm=l1_post_norm),
        dict(pre_norm=l2_pre_norm, w_in=l2_w_in, lambda_q1=l2_lambda_q1, lambda_k1=l2_lambda_k1,
             lambda_q2=l2_lambda_q2, lambda_k2=l2_lambda_k2, subln=l2_subln,
             w_out=l2_w_out, post_norm=l2_post_norm),
        dict(pre_norm=l3_pre_norm, w_in=l3_w_in, q_norm=l3_q_norm, w_uq=l3_w_uq,
             kv_norm=l3_kv_norm, w_ukv=l3_w_ukv, w_out=l3_w_out, post_norm=l3_post_norm),
    ]
    return (_trunk(x_prompt, layers), _trunk(x_sample, layers))
```
